```python
import math
import jax, jax.numpy as jnp
from jax import lax
import numpy as np

D_MODEL = 4096
BATCH = 2
SEQ = 4096
DEPTH = 4
DEC_BATCH = 2
DEC_SEQ = 8192
PAST_LEN = 128

GRID_W = 64
HEAD_DIM = 128
NA_HEADS = D_MODEL // 512
NA_WIDTH = NA_HEADS * HEAD_DIM
NA_WIN_ROWS = 8
NA_WIN_COLS = 16
DIFF_HEADS = D_MODEL // 512
DIFF_QK_DIM = HEAD_DIM
DIFF_V_DIM = 2 * DIFF_QK_DIM
DIFF_WIDTH = DIFF_HEADS * DIFF_V_DIM
CONV_WIDTH = D_MODEL - NA_WIDTH - DIFF_WIDTH
CONV_K = 3
MIX_WIDTH = NA_WIDTH + DIFF_WIDTH + CONV_WIDTH
IN_WIDTH = 3 * NA_WIDTH + 3 * DIFF_WIDTH + 3 * CONV_WIDTH
D_FF = 4 * D_MODEL
ROPE_THETA = 500000.0
ROT_DIM = DIFF_QK_DIM // 4
Q_BLOCK = 128
NORM_EPS = 1e-5
NEG_INF = -1e30

kernel_name = "hymba_style_na_diffattn_shortconv_encoder"


def rms_norm(x, g):
    xf = x.astype(jnp.float32)
    y = xf * lax.rsqrt(jnp.mean(xf * xf, axis=-1, keepdims=True) + NORM_EPS)
    return y.astype(x.dtype) * g


def neighbourhood_attention(q, k, v, rpb):
    b, s, h, d = q.shape
    rows = s // GRID_W
    wr = min(NA_WIN_ROWS, rows)
    wc = NA_WIN_COLS
    qg = (q * (d ** -0.5)).reshape(b, rows, GRID_W, h, d)
    kg = k.reshape(b, rows, GRID_W, h, d)
    vg = v.reshape(b, rows, GRID_W, h, d)
    r = jnp.arange(rows)
    r0 = jnp.clip(r - wr // 2, 0, rows - wr)
    band = r0[:, None] + jnp.arange(wr)[None, :]
    k_band = kg[:, band]
    v_band = vg[:, band]
    c = jnp.arange(GRID_W)
    c0 = jnp.clip(c - wc // 2, 0, GRID_W - wc)
    col_in = (c[None, :] >= c0[:, None]) & (c[None, :] < c0[:, None] + wc)
    dr = band - r[:, None] + (NA_WIN_ROWS - 1)
    dc = jnp.clip(c[None, :] - c[:, None], -(wc - 1), wc - 1) + (wc - 1)
    bias = rpb[:, dr[:, None, :, None], dc[None, :, None, :]]
    sc = jnp.einsum('brqhd,brjkhd->bhrqjk', qg, k_band).astype(jnp.float32) + bias.astype(jnp.float32)
    sc = jnp.where(col_in[:, None, :], sc, NEG_INF)
    p = jax.nn.softmax(sc.reshape(b, h, rows, GRID_W, wr * GRID_W), axis=-1).reshape(sc.shape)
    o = jnp.einsum('bhrqjk,brjkhd->brqhd', p.astype(v.dtype), v_band)
    return o.reshape(b, s, h * d)


def partial_rope(x):
    s = x.shape[1]
    half = ROT_DIM // 2
    inv_freq = ROPE_THETA ** (-jnp.arange(half, dtype=jnp.float32) / half)
    ang = jnp.arange(s, dtype=jnp.float32)[:, None] * inv_freq[None, :]
    cos = jnp.cos(ang)[None, :, None, None, :].astype(x.dtype)
    sin = jnp.sin(ang)[None, :, None, None, :].astype(x.dtype)
    x1 = x[..., :half]
    x2 = x[..., half:ROT_DIM]
    return jnp.concatenate([x1 * cos - x2 * sin, x2 * cos + x1 * sin, x[..., ROT_DIM:]], axis=-1)


def differential_attention(q, k, v, lam, lam_init, subln_g):
    b, s, h, _, d = q.shape
    q = partial_rope(q) * (d ** -0.5)
    k = partial_rope(k)
    nb = s // Q_BLOCK
    qb = jnp.moveaxis(q.reshape(b, nb, Q_BLOCK, h, 2, d), 1, 0)

    def block(qblk):
        sc = jnp.einsum('bqhcd,bkhcd->bhcqk', qblk, k).astype(jnp.float32)
        p = jax.nn.softmax(sc, axis=-1)
        a = p[:, :, 0] - lam * p[:, :, 1]
        return jnp.einsum('bhqk,bkhe->bqhe', a.astype(v.dtype), v)

    o = lax.map(block, qb)
    o = jnp.moveaxis(o, 0, 1).reshape(b, s, h, 2 * d)
    o = rms_norm(o, subln_g) * (1.0 - lam_init)
    return o.reshape(b, s, h * 2 * d)


def short_conv_mixer(u, b_gate, c_gate, conv_w):
    z = c_gate * u
    zp = jnp.pad(z, ((0, 0), (1, 1), (0, 0)))
    y = conv_w[0] * zp[:, :-2] + conv_w[1] * zp[:, 1:-1] + conv_w[2] * zp[:, 2:]
    return b_gate * y


def trunk_layer(x, layer_idx, norm1_g, w_in, na_rpb, lq1, lk1, lq2, lk2, subln_g, conv_w,
                w_out, norm2_g, w_up, w_down):
    b, s, _ = x.shape
    h = rms_norm(x, norm1_g)
    proj = h @ w_in
    splits = np.cumsum([3 * NA_WIDTH, DIFF_WIDTH, DIFF_WIDTH, DIFF_WIDTH, CONV_WIDTH, CONV_WIDTH]).tolist()
    na_qkv, dq, dk, dv, cu, cb, cc = jnp.split(proj, splits, axis=-1)
    na_qkv = na_qkv.reshape(b, s, 3, NA_HEADS, HEAD_DIM)
    na_out = neighbourhood_attention(na_qkv[:, :, 0], na_qkv[:, :, 1], na_qkv[:, :, 2], na_rpb)
    lam_init = 0.8 - 0.6 * math.exp(-0.3 * layer_idx)
    lam = (jnp.exp(jnp.sum(lq1.astype(jnp.float32) * lk1.astype(jnp.float32)))
           - jnp.exp(jnp.sum(lq2.astype(jnp.float32) * lk2.astype(jnp.float32))) + lam_init)
    diff_out = differential_attention(dq.reshape(b, s, DIFF_HEADS, 2, DIFF_QK_DIM),
                                      dk.reshape(b, s, DIFF_HEADS, 2, DIFF_QK_DIM),
                                      dv.reshape(b, s, DIFF_HEADS, DIFF_V_DIM),
                                      lam, lam_init, subln_g)
    conv_out = short_conv_mixer(cu, cb, cc, conv_w)
    x = x + jnp.concatenate([na_out, diff_out, conv_out], axis=-1) @ w_out
    up = rms_norm(x, norm2_g) @ w_up
    return x + jnp.square(jax.nn.relu(up)) @ w_down


def run_trunk(x, norm1_g, w_in, na_rpb, diff_lq1, diff_lk1, diff_lq2, diff_lk2, diff_subln_g,
              conv_w, w_out, norm2_g, w_up, w_down, final_norm_g):
    for i in range(DEPTH):
        x = trunk_layer(x, i, norm1_g[i], w_in[i], na_rpb[i], diff_lq1[i], diff_lk1[i], diff_lq2[i],
                        diff_lk2[i], diff_subln_g[i], conv_w[i], w_out[i], norm2_g[i], w_up[i], w_down[i])
    return rms_norm(x, final_norm_g)


def setup_inputs(seed: int = 0) -> dict:
    key = jax.random.key(seed)
    ks = jax.random.split(key, 17)

    def nrm(k, shape, scale):
        return jax.random.normal(k, shape, jnp.float32) * scale

    return {
        "x_prompt": nrm(ks[0], (BATCH, SEQ, D_MODEL), 1.0),
        "x_sample": nrm(ks[1], (DEC_BATCH, DEC_SEQ, D_MODEL), 1.0),
        "norm1_g": 1.0 + nrm(ks[2], (DEPTH, D_MODEL), 0.02),
        "w_in": nrm(ks[3], (DEPTH, D_MODEL, IN_WIDTH), D_MODEL ** -0.5),
        "na_rpb": nrm(ks[4], (DEPTH, NA_HEADS, 2 * NA_WIN_ROWS - 1, 2 * NA_WIN_COLS - 1), 0.1),
        "diff_lq1": nrm(ks[5], (DEPTH, DIFF_QK_DIM), 0.1),
        "diff_lk1": nrm(ks[6], (DEPTH, DIFF_QK_DIM), 0.1),
        "diff_lq2": nrm(ks[7], (DEPTH, DIFF_QK_DIM), 0.1),
        "diff_lk2": nrm(ks[8], (DEPTH, DIFF_QK_DIM), 0.1),
        "diff_subln_g": 1.0 + nrm(ks[9], (DEPTH, DIFF_V_DIM), 0.02),
        "conv_w": nrm(ks[10], (DEPTH, CONV_K, CONV_WIDTH), CONV_K ** -0.5),
        "w_out": nrm(ks[11], (DEPTH, MIX_WIDTH, D_MODEL), MIX_WIDTH ** -0.5),
        "norm2_g": 1.0 + nrm(ks[12], (DEPTH, D_MODEL), 0.02),
        "w_up": nrm(ks[13], (DEPTH, D_MODEL, D_FF), D_MODEL ** -0.5),
        "w_down": nrm(ks[14], (DEPTH, D_FF, D_MODEL), D_FF ** -0.5),
        "final_norm_g": 1.0 + nrm(ks[15], (D_MODEL,), 0.02),
    }


def reference(x_prompt, x_sample, norm1_g, w_in, na_rpb, diff_lq1, diff_lk1, diff_lq2, diff_lk2,
              diff_subln_g, conv_w, w_out, norm2_g, w_up, w_down, final_norm_g):
    y_prompt = run_trunk(x_prompt, norm1_g, w_in, na_rpb, diff_lq1, diff_lk1, diff_lq2, diff_lk2,
                         diff_subln_g, conv_w, w_out, norm2_g, w_up, w_down, final_norm_g)
    y_sample = run_trunk(x_sample, norm1_g, w_in, na_rpb, diff_lq1, diff_lk1, diff_lq2, diff_lk2,
                         diff_subln_g, conv_w, w_out, norm2_g, w_up, w_down, final_norm_g)
    return (y_prompt, y_sample)
```

```python
import functools
import math

import jax
import jax.numpy as jnp
from jax import lax
from jax.experimental import pallas as pl
from jax.experimental.pallas import tpu as pltpu

F32 = jnp.float32
BF16 = jnp.bfloat16

D_MODEL = 4096
DEPTH = 4
GRID_W = 64
HEAD_DIM = 128
NA_HEADS = D_MODEL // 512
NA_WIDTH = NA_HEADS * HEAD_DIM
NA_WIN_ROWS = 8
NA_WIN_COLS = 16
DIFF_HEADS = D_MODEL // 512
DIFF_QK_DIM = HEAD_DIM
DIFF_V_DIM = 2 * DIFF_QK_DIM
DIFF_WIDTH = DIFF_HEADS * DIFF_V_DIM
CONV_WIDTH = D_MODEL - NA_WIDTH - DIFF_WIDTH
IN_WIDTH = 3 * NA_WIDTH + 3 * DIFF_WIDTH + 3 * CONV_WIDTH
D_FF = 4 * D_MODEL
ROPE_THETA = 500000.0
ROT_DIM = DIFF_QK_DIM // 4
NORM_EPS = 1e-5
NEG_INF = -1e30

NA_Q_COL = 0
NA_K_COL = NA_WIDTH
NA_V_COL = 2 * NA_WIDTH
DIFF_Q_COL = 3 * NA_WIDTH
DIFF_K_COL = DIFF_Q_COL + DIFF_WIDTH
DIFF_V_COL = DIFF_K_COL + DIFF_WIDTH
CONV_U_COL = DIFF_V_COL + DIFF_WIDTH
CONV_B_COL = CONV_U_COL + CONV_WIDTH
CONV_C_COL = CONV_B_COL + CONV_WIDTH

V7X_VMEM_LIMIT_BYTES = 56 * 1024 * 1024
LANES = 128


def _params(*sem):
    return pltpu.CompilerParams(dimension_semantics=sem, vmem_limit_bytes=V7X_VMEM_LIMIT_BYTES)


def _rmsnorm_kernel(x_ref, g_ref, o_ref):
    x = x_ref[...]
    ms = jnp.mean(x * x, axis=-1, keepdims=True)
    o_ref[...] = ((x * lax.rsqrt(ms + NORM_EPS)) * g_ref[...]).astype(o_ref.dtype)


def _rmsnorm(x, g, out_dtype, block_rows=256):
    t, d = x.shape
    return pl.pallas_call(
        _rmsnorm_kernel,
        grid=(t // block_rows,),
        in_specs=[pl.BlockSpec((block_rows, d), lambda i: (i, 0)),
                  pl.BlockSpec((1, d), lambda i: (0, 0))],
        out_specs=pl.BlockSpec((block_rows, d), lambda i: (i, 0)),
        out_shape=jax.ShapeDtypeStruct((t, d), out_dtype),
        compiler_params=_params("parallel"),
        name="rmsnorm",
    )(x, g.reshape(1, d))


def _mm_kernel(*refs, k_sizes, has_res, relu2):
    n_a = len(k_sizes)
    a_refs = refs[:n_a]
    w_ref = refs[n_a]
    o_ref = refs[-1]
    acc = None
    k0 = 0
    for a_ref, ks in zip(a_refs, k_sizes):
        d = jnp.dot(a_ref[...], w_ref[k0:k0 + ks, :], preferred_element_type=F32)
        acc = d if acc is None else acc + d
        k0 += ks
    if relu2:
        r = jnp.maximum(acc, 0.0)
        acc = r * r
    if has_res:
        acc = acc + refs[n_a + 1][...]
    o_ref[...] = acc.astype(o_ref.dtype)


def _matmul(a_list, w, out_dtype, res=None, relu2=False, bm=1024, bn=1024, name="matmul"):
    m = a_list[0].shape[0]
    k_sizes = tuple(a.shape[1] for a in a_list)
    k, n = w.shape
    assert sum(k_sizes) == k
    in_specs = [pl.BlockSpec((bm, ks), lambda i, j: (i, 0)) for ks in k_sizes]
    in_specs.append(pl.BlockSpec((k, bn), lambda i, j: (0, j)))
    args = list(a_list) + [w]
    if res is not None:
        in_specs.append(pl.BlockSpec((bm, bn), lambda i, j: (i, j)))
        args.append(res)
    return pl.pallas_call(
        functools.partial(_mm_kernel, k_sizes=k_sizes, has_res=res is not None, relu2=relu2),
        grid=(m // bm, n // bn),
        in_specs=in_specs,
        out_specs=pl.BlockSpec((bm, bn), lambda i, j: (i, j)),
        out_shape=jax.ShapeDtypeStruct((m, n), out_dtype),
        compiler_params=_params("parallel", "arbitrary"),
        name=name,
    )(*args)


def _mm_kgrid_kernel(a_ref, w_ref, res_ref, o_ref):
    d = jnp.dot(a_ref[...], w_ref[...], preferred_element_type=F32)

    @pl.when(pl.program_id(2) == 0)
    def _():
        o_ref[...] = res_ref[...] + d

    @pl.when(pl.program_id(2) != 0)
    def _():
        o_ref[...] += d


def _matmul_kgrid(a, w, res, bm=1024, bn=1024, bk=2048, name="matmul_kgrid"):
    m, k = a.shape
    n = w.shape[1]
    return pl.pallas_call(
        _mm_kgrid_kernel,
        grid=(m // bm, n // bn, k // bk),
        in_specs=[pl.BlockSpec((bm, bk), lambda i, j, kk: (i, kk)),
                  pl.BlockSpec((bk, bn), lambda i, j, kk: (kk, j)),
                  pl.BlockSpec((bm, bn), lambda i, j, kk: (i, j))],
        out_specs=pl.BlockSpec((bm, bn), lambda i, j, kk: (i, j)),
        out_shape=jax.ShapeDtypeStruct((m, n), F32),
        compiler_params=_params("parallel", "parallel", "arbitrary"),
        name=name,
    )(a, w, res)


NA_BIAS_VARIANTS = NA_WIN_ROWS
NA_BAND_KEYS = NA_WIN_ROWS * GRID_W
RPB_ROWS = 2 * NA_WIN_ROWS - 1
RPB_COLS = 2 * NA_WIN_COLS - 1


def _na_bias_kernel(rpb_ref, o_ref):
    h = pl.program_id(0)
    shape = (GRID_W, 2 * GRID_W)
    lane = lax.broadcasted_iota(jnp.int32, shape, 1)
    cq = lax.broadcasted_iota(jnp.int32, shape, 0)
    ck = lane & (GRID_W - 1)
    upper = lane >= GRID_W
    dcl = jnp.clip(ck - cq, -(NA_WIN_COLS - 1), NA_WIN_COLS - 1) + (NA_WIN_COLS - 1)
    c0 = jnp.clip(cq - NA_WIN_COLS // 2, 0, GRID_W - NA_WIN_COLS)
    col_in = jnp.logical_and(ck >= c0, ck < c0 + NA_WIN_COLS)
    pairs = []
    for d in range(RPB_ROWS - 1):
        acc = jnp.zeros(shape, F32)
        for dc in range(RPB_COLS):
            lo = rpb_ref[(h * RPB_ROWS + d) * RPB_COLS + dc]
            hi = rpb_ref[(h * RPB_ROWS + d + 1) * RPB_COLS + dc]
            acc = jnp.where(dcl == dc, jnp.where(upper, hi, lo), acc)
        pairs.append(jnp.where(col_in, acc, NEG_INF))
    for off in range(NA_BIAS_VARIANTS):
        for jj in range(NA_WIN_ROWS // 2):
            o_ref[0, off, :, jj * 2 * GRID_W:(jj + 1) * 2 * GRID_W] = pairs[off + 2 * jj]


def _na_bias(rpb):
    return pl.pallas_call(
        _na_bias_kernel,
        grid=(NA_HEADS,),
        in_specs=[pl.BlockSpec(memory_space=pltpu.SMEM)],
        out_specs=pl.BlockSpec((1, NA_BIAS_VARIANTS, GRID_W, NA_BAND_KEYS), lambda h: (h, 0, 0, 0)),
        out_shape=jax.ShapeDtypeStruct((NA_HEADS, NA_BIAS_VARIANTS, GRID_W, NA_BAND_KEYS), F32),
        compiler_params=_params("arbitrary"),
        name="na_bias",
    )(rpb.reshape(-1))


def _na_kernel(q_ref, k_ref, v_ref, bias_ref, o_ref, kb_ref, vb_ref, *, seq):
    rows = seq // GRID_W
    cast_rows = 512

    def cast_body(c, carry):
        sl = pl.ds(pl.multiple_of(c * cast_rows, cast_rows), cast_rows)
        kb_ref[sl, :] = k_ref[sl, :].astype(BF16)
        vb_ref[sl, :] = v_ref[sl, :].astype(BF16)
        return carry

    lax.fori_loop(0, seq // cast_rows, cast_body, 0)
    scale = HEAD_DIM ** -0.5

    def row_body(r, carry):
        r0 = jnp.clip(r - NA_WIN_ROWS // 2, 0, rows - NA_WIN_ROWS)
        off = r0 - r + (NA_WIN_ROWS - 1)
        qsl = pl.ds(pl.multiple_of(r * GRID_W, GRID_W), GRID_W)
        ksl = pl.ds(pl.multiple_of(r0 * GRID_W, GRID_W), NA_BAND_KEYS)
        q = (q_ref[qsl, :] * scale).astype(BF16)
        s = lax.dot_general(q, kb_ref[ksl, :], (((1,), (1,)), ((), ())), preferred_element_type=F32)
        s = s + bias_ref[0, off]
        m = jnp.max(s, axis=-1, keepdims=True)
        p = jnp.exp(s - m)
        l = jnp.sum(p, axis=-1, keepdims=True)
        o = jnp.dot(p.astype(BF16), vb_ref[ksl, :], preferred_element_type=F32)
        o_ref[qsl, :] = (o * (1.0 / l)).astype(o_ref.dtype)
        return carry

    lax.fori_loop(0, rows, row_body, 0)


def _na_attention(proj, bias, batch, seq):
    t = proj.shape[0]
    qb, kb, vb = NA_Q_COL // HEAD_DIM, NA_K_COL // HEAD_DIM, NA_V_COL // HEAD_DIM
    return pl.pallas_call(
        functools.partial(_na_kernel, seq=seq),
        grid=(batch, NA_HEADS),
        in_specs=[pl.BlockSpec((seq, HEAD_DIM), lambda b, h: (b, qb + h)),
                  pl.BlockSpec((seq, HEAD_DIM), lambda b, h: (b, kb + h)),
                  pl.BlockSpec((seq, HEAD_DIM), lambda b, h: (b, vb + h)),
                  pl.BlockSpec((1, NA_BIAS_VARIANTS, GRID_W, NA_BAND_KEYS), lambda b, h: (h, 0, 0, 0))],
        out_specs=pl.BlockSpec((seq, HEAD_DIM), lambda b, h: (b, h)),
        out_shape=jax.ShapeDtypeStruct((t, NA_WIDTH), BF16),
        scratch_shapes=[pltpu.VMEM((seq, HEAD_DIM), BF16), pltpu.VMEM((seq, HEAD_DIM), BF16)],
        compiler_params=_params("parallel", "parallel"),
        name="na_attention",
    )(proj, proj, proj, bias)


DIFF_PREP_ROWS = 512
DIFF_PREP_COLS = 1024
DIFF_Q_ROWS = 256


def _rope_tables(seq):
    half = ROT_DIM // 2
    inv_freq = ROPE_THETA ** (-jnp.arange(half, dtype=F32) / half)
    ang = jnp.arange(seq, dtype=F32)[:, None] * inv_freq[None, :]
    cos, sin = jnp.cos(ang), jnp.sin(ang)
    rest = DIFF_QK_DIM - ROT_DIM
    c = jnp.concatenate([cos, cos, jnp.ones((seq, rest), F32)], axis=1)
    sa = jnp.concatenate([-sin, jnp.zeros((seq, DIFF_QK_DIM - half), F32)], axis=1)
    sb = jnp.concatenate([jnp.zeros((seq, half), F32), sin, jnp.zeros((seq, rest), F32)], axis=1)
    return c, sa, sb


def _diff_prep_kernel(q_ref, k_ref, v_ref, c_ref, sa_ref, sb_ref, qo_ref, kt_ref, vo_ref):
    half = ROT_DIM // 2
    c, sa, sb = c_ref[...], sa_ref[...], sb_ref[...]
    scale = DIFF_QK_DIM ** -0.5

    def rope(x):
        return x * c + pltpu.roll(x, DIFF_QK_DIM - half, 1) * sa + pltpu.roll(x, half, 1) * sb

    for g in range(DIFF_PREP_COLS // DIFF_QK_DIM):
        sl = slice(g * DIFF_QK_DIM, (g + 1) * DIFF_QK_DIM)
        qo_ref[:, sl] = (rope(q_ref[:, sl]) * scale).astype(BF16)
        kt_ref[0, 0, sl, :] = rope(k_ref[:, sl]).T.astype(BF16)
    vo_ref[...] = v_ref[...].astype(BF16)


def _diff_prep(proj, batch, seq):
    t = proj.shape[0]
    rb, cb = DIFF_PREP_ROWS, DIFF_PREP_COLS
    nsb = seq // rb
    c, sa, sb = _rope_tables(seq)
    qc, kc, vc = DIFF_Q_COL // cb, DIFF_K_COL // cb, DIFF_V_COL // cb
    tab_spec = pl.BlockSpec((rb, DIFF_QK_DIM), lambda i, j: (i % nsb, 0))
    return pl.pallas_call(
        _diff_prep_kernel,
        grid=(t // rb, DIFF_WIDTH // cb),
        in_specs=[pl.BlockSpec((rb, cb), lambda i, j: (i, qc + j)),
                  pl.BlockSpec((rb, cb), lambda i, j: (i, kc + j)),
                  pl.BlockSpec((rb, cb), lambda i, j: (i, vc + j)),
                  tab_spec, tab_spec, tab_spec],
        out_specs=[pl.BlockSpec((rb, cb), lambda i, j: (i, j)),
                   pl.BlockSpec((1, 1, cb, rb), lambda i, j: (i // nsb, i % nsb, j, 0)),
                   pl.BlockSpec((rb, cb), lambda i, j: (i, j))],
        out_shape=[jax.ShapeDtypeStruct((t, DIFF_WIDTH), BF16),
                   jax.ShapeDtypeStruct((batch, nsb, DIFF_WIDTH, rb), BF16),
                   jax.ShapeDtypeStruct((t, DIFF_WIDTH), BF16)],
        compiler_params=_params("parallel", "parallel"),
        name="diff_prep",
    )(proj, proj, proj, c, sa, sb)


def _diff_kernel(lam_ref, q_ref, kt_ref, v_ref, lq1_ref, lk1_ref, lq2_ref, lk2_ref, g_ref, o_ref, *, seq):
    bk = DIFF_PREP_ROWS
    bq = q_ref.shape[0]
    outs = []
    for c in range(2):
        q = q_ref[:, c * DIFF_QK_DIM:(c + 1) * DIFF_QK_DIM]

        def body(j, carry, c=c, q=q):
            m, l, acc = carry
            kt = kt_ref[0, j, c * DIFF_QK_DIM:(c + 1) * DIFF_QK_DIM, :]
            s = jnp.dot(q, kt, preferred_element_type=F32)
            m_new = jnp.maximum(m, jnp.max(s, axis=-1, keepdims=True))
            alpha = jnp.exp(m - m_new)
            p = jnp.exp(s - m_new)
            l = alpha * l + jnp.sum(p, axis=-1, keepdims=True)
            vb = v_ref[pl.ds(pl.multiple_of(j * bk, bk), bk), :]
            acc = alpha * acc + jnp.dot(p.astype(BF16), vb, preferred_element_type=F32)
            return m_new, l, acc

        init = (jnp.full((bq, 1), NEG_INF, F32), jnp.zeros((bq, 1), F32), jnp.zeros((bq, DIFF_V_DIM), F32))
        _, l, acc = lax.fori_loop(0, seq // bk, body, init)
        outs.append(acc * (1.0 / l))
    lam_init = lam_ref[0]
    lam = (jnp.exp(jnp.sum(lq1_ref[...] * lk1_ref[...], axis=-1, keepdims=True))
           - jnp.exp(jnp.sum(lq2_ref[...] * lk2_ref[...], axis=-1, keepdims=True)) + lam_init)
    o = outs[0] - lam * outs[1]
    ms = jnp.mean(o * o, axis=-1, keepdims=True)
    y = (o * lax.rsqrt(ms + NORM_EPS)) * g_ref[...]
    o_ref[...] = (y * lam_ref[1]).astype(o_ref.dtype)


def _diff_attention(q, kt, v, lam_consts, lq1, lk1, lq2, lk2, subln_g, batch, seq):
    t = q.shape[0]
    bq = DIFF_Q_ROWS
    nqb = seq // bq
    nkb = seq // DIFF_PREP_ROWS
    vec = lambda n: pl.BlockSpec((1, n), lambda b, h, i: (0, 0))
    return pl.pallas_call(
        functools.partial(_diff_kernel, seq=seq),
        grid=(batch, DIFF_HEADS, nqb),
        in_specs=[pl.BlockSpec(memory_space=pltpu.SMEM),
                  pl.BlockSpec((bq, DIFF_V_DIM), lambda b, h, i: (b * nqb + i, h)),
                  pl.BlockSpec((1, nkb, DIFF_V_DIM, DIFF_PREP_ROWS), lambda b, h, i: (b, 0, h, 0)),
                  pl.BlockSpec((seq, DIFF_V_DIM), lambda b, h, i: (b, h)),
                  vec(DIFF_QK_DIM), vec(DIFF_QK_DIM), vec(DIFF_QK_DIM), vec(DIFF_QK_DIM), vec(DIFF_V_DIM)],
        out_specs=pl.BlockSpec((bq, DIFF_V_DIM), lambda b, h, i: (b * nqb + i, h)),
        out_shape=jax.ShapeDtypeStruct((t, DIFF_WIDTH), BF16),
        compiler_params=_params("parallel", "parallel", "arbitrary"),
        name="diff_attention",
    )(lam_consts, q, kt, v, lq1.reshape(1, -1), lk1.reshape(1, -1), lq2.reshape(1, -1), lk2.reshape(1, -1),
      subln_g.reshape(1, -1))


CONV_CHUNK = 512


def _conv_kernel(u_ref, b_ref, c_ref, w_ref, o_ref, *, seq):
    n_chunks = seq // CONV_CHUNK
    w0, w1, w2 = w_ref[0:1, :], w_ref[1:2, :], w_ref[2:3, :]
    row = lax.broadcasted_iota(jnp.int32, (CONV_CHUNK, LANES), 0)

    def body(ci, carry):
        t0 = pl.multiple_of(ci * CONV_CHUNK, CONV_CHUNK)
        sl = pl.ds(t0, CONV_CHUNK)
        z = c_ref[sl, :] * u_ref[sl, :]
        psl = pl.ds(pl.multiple_of(jnp.maximum(t0 - 8, 0), 8), 8)
        nsl = pl.ds(pl.multiple_of(jnp.minimum(t0 + CONV_CHUNK, seq - 8), 8), 8)
        zp = (c_ref[psl, :] * u_ref[psl, :])[7:8, :]
        zn = (c_ref[nsl, :] * u_ref[nsl, :])[0:1, :]
        zp = jnp.where(ci > 0, zp, 0.0)
        zn = jnp.where(ci < n_chunks - 1, zn, 0.0)
        z_prev = jnp.where(row == 0, zp, pltpu.roll(z, 1, 0))
        z_next = jnp.where(row == CONV_CHUNK - 1, zn, pltpu.roll(z, CONV_CHUNK - 1, 0))
        y = w0 * z_prev + w1 * z + w2 * z_next
        o_ref[sl, :] = (b_ref[sl, :] * y).astype(o_ref.dtype)
        return carry

    lax.fori_loop(0, n_chunks, body, 0)


def _short_conv(proj, conv_w, batch, seq):
    t = proj.shape[0]
    ub, bb, cb = CONV_U_COL // LANES, CONV_B_COL // LANES, CONV_C_COL // LANES
    return pl.pallas_call(
        functools.partial(_conv_kernel, seq=seq),
        grid=(batch, CONV_WIDTH // LANES),
        in_specs=[pl.BlockSpec((seq, LANES), lambda b, j: (b, ub + j)),
                  pl.BlockSpec((seq, LANES), lambda b, j: (b, bb + j)),
                  pl.BlockSpec((seq, LANES), lambda b, j: (b, cb + j)),
                  pl.BlockSpec((3, LANES), lambda b, j: (0, j))],
        out_specs=pl.BlockSpec((seq, LANES), lambda b, j: (b, j)),
        out_shape=jax.ShapeDtypeStruct((t, CONV_WIDTH), BF16),
        compiler_params=_params("parallel", "parallel"),
        name="short_conv",
    )(proj, proj, proj, conv_w)


def _trunk_layer(x, layer_idx, batch, seq, norm1_g, w_in, na_bias, lq1, lk1, lq2, lk2, subln_g, conv_w,
                 w_out, norm2_g, w_up, w_down):
    h = _rmsnorm(x, norm1_g, BF16)
    proj = _matmul([h], w_in, F32, name="in_proj")
    na_out = _na_attention(proj, na_bias, batch, seq)
    lam_init = 0.8 - 0.6 * math.exp(-0.3 * layer_idx)
    lam_consts = jnp.array([lam_init, 1.0 - lam_init], F32)
    dq, dkt, dv = _diff_prep(proj, batch, seq)
    diff_out = _diff_attention(dq, dkt, dv, lam_consts, lq1, lk1, lq2, lk2, subln_g, batch, seq)
    conv_out = _short_conv(proj, conv_w, batch, seq)
    x = _matmul([na_out, diff_out, conv_out], w_out, F32, res=x, bm=512, name="out_proj")
    h2 = _rmsnorm(x, norm2_g, BF16)
    act = _matmul([h2], w_up, BF16, relu2=True, name="mlp_up")
    return _matmul_kgrid(act, w_down, x, name="mlp_down")


def _run_trunk(x, layers, final_norm_g):
    batch, seq, d = x.shape
    x = x.reshape(batch * seq, d)
    for i, lw in enumerate(layers):
        x = _trunk_layer(x, i, batch, seq, *lw)
    return _rmsnorm(x, final_norm_g, F32).reshape(batch, seq, d)


def kernel(x_prompt, x_sample, norm1_g, w_in, na_rpb, diff_lq1, diff_lk1, diff_lq2, diff_lk2, diff_subln_g,
           conv_w, w_out, norm2_g, w_up, w_down, final_norm_g):
    layers = []
    for i in range(DEPTH):
        layers.append((norm1_g[i], w_in[i].astype(BF16), _na_bias(na_rpb[i]), diff_lq1[i], diff_lk1[i],
                       diff_lq2[i], diff_lk2[i], diff_subln_g[i], conv_w[i], w_out[i].astype(BF16),
                       norm2_g[i], w_up[i].astype(BF16), w_down[i].astype(BF16)))
    return (_run_trunk(x_prompt, layers, final_norm_g), _run_trunk(x_sample, layers, final_norm_g))
```

```python
import functools
import math

import jax
import jax.numpy as jnp
from jax import lax
from jax.experimental import pallas as pl
from jax.experimental.pallas import tpu as pltpu

F32 = jnp.float32
BF16 = jnp.bfloat16

D_MODEL = 4096
DEPTH = 4
GRID_W = 64
HEAD_DIM = 128
NA_HEADS = D_MODEL // 512
NA_WIDTH = NA_HEADS * HEAD_DIM
NA_WIN_ROWS = 8
NA_WIN_COLS = 16
DIFF_HEADS = D_MODEL // 512
DIFF_QK_DIM = HEAD_DIM
DIFF_V_DIM = 2 * DIFF_QK_DIM
DIFF_WIDTH = DIFF_HEADS * DIFF_V_DIM
CONV_WIDTH = D_MODEL - NA_WIDTH - DIFF_WIDTH
IN_WIDTH = 3 * NA_WIDTH + 3 * DIFF_WIDTH + 3 * CONV_WIDTH
D_FF = 4 * D_MODEL
ROPE_THETA = 500000.0
ROT_DIM = DIFF_QK_DIM // 4
NORM_EPS = 1e-5
NEG_INF = -1e30

NA_Q_COL = 0
NA_K_COL = NA_WIDTH
NA_V_COL = 2 * NA_WIDTH
DIFF_Q_COL = 3 * NA_WIDTH
DIFF_K_COL = DIFF_Q_COL + DIFF_WIDTH
DIFF_V_COL = DIFF_K_COL + DIFF_WIDTH
CONV_U_COL = DIFF_V_COL + DIFF_WIDTH
CONV_B_COL = CONV_U_COL + CONV_WIDTH
CONV_C_COL = CONV_B_COL + CONV_WIDTH

V7X_VMEM_LIMIT_BYTES = 56 * 1024 * 1024
LANES = 128


def _params(*sem):
    return pltpu.CompilerParams(dimension_semantics=sem, vmem_limit_bytes=V7X_VMEM_LIMIT_BYTES)


def _rmsnorm_kernel(x_ref, g_ref, o_ref):
    x = x_ref[...]
    ms = jnp.mean(x * x, axis=-1, keepdims=True)
    o_ref[...] = ((x * lax.rsqrt(ms + NORM_EPS)) * g_ref[...]).astype(o_ref.dtype)


def _rmsnorm(x, g, out_dtype, block_rows=256):
    t, d = x.shape
    return pl.pallas_call(
        _rmsnorm_kernel,
        grid=(t // block_rows,),
        in_specs=[pl.BlockSpec((block_rows, d), lambda i: (i, 0)),
                  pl.BlockSpec((1, d), lambda i: (0, 0))],
        out_specs=pl.BlockSpec((block_rows, d), lambda i: (i, 0)),
        out_shape=jax.ShapeDtypeStruct((t, d), out_dtype),
        compiler_params=_params("parallel"),
        name="rmsnorm",
    )(x, g.reshape(1, d))


def _mm_kernel(*refs, k_sizes, has_res, relu2):
    n_a = len(k_sizes)
    a_refs = refs[:n_a]
    w_ref = refs[n_a]
    o_ref = refs[-1]
    acc = None
    k0 = 0
    for a_ref, ks in zip(a_refs, k_sizes):
        d = jnp.dot(a_ref[...], w_ref[k0:k0 + ks, :], preferred_element_type=F32)
        acc = d if acc is None else acc + d
        k0 += ks
    if relu2:
        r = jnp.maximum(acc, 0.0)
        acc = r * r
    if has_res:
        acc = acc + refs[n_a + 1][...]
    o_ref[...] = acc.astype(o_ref.dtype)


def _matmul(a_list, w, layer, out_dtype, res=None, relu2=False, bm=1024, bn=1024, name="matmul"):
    m = a_list[0].shape[0]
    k_sizes = tuple(a.shape[1] for a in a_list)
    _, k, n = w.shape
    assert sum(k_sizes) == k
    in_specs = [pl.BlockSpec((bm, ks), lambda i, j: (i, 0)) for ks in k_sizes]
    in_specs.append(pl.BlockSpec((None, k, bn), lambda i, j: (layer, 0, j)))
    args = list(a_list) + [w]
    if res is not None:
        in_specs.append(pl.BlockSpec((bm, bn), lambda i, j: (i, j)))
        args.append(res)
    return pl.pallas_call(
        functools.partial(_mm_kernel, k_sizes=k_sizes, has_res=res is not None, relu2=relu2),
        grid=(m // bm, n // bn),
        in_specs=in_specs,
        out_specs=pl.BlockSpec((bm, bn), lambda i, j: (i, j)),
        out_shape=jax.ShapeDtypeStruct((m, n), out_dtype),
        compiler_params=_params("parallel", "arbitrary"),
        name=name,
    )(*args)


def _mm_kgrid_kernel(a_ref, w_ref, res_ref, o_ref):
    @pl.when(pl.program_id(2) == 0)
    def _():
        o_ref[...] = res_ref[...]

    o_ref[...] += jnp.dot(a_ref[...], w_ref[...], preferred_element_type=F32)


def _matmul_kgrid(a, w, layer, res, bm=1024, bn=1024, bk=4096, name="matmul_kgrid"):
    m, k = a.shape
    n = w.shape[2]
    return pl.pallas_call(
        _mm_kgrid_kernel,
        grid=(m // bm, n // bn, k // bk),
        in_specs=[pl.BlockSpec((bm, bk), lambda i, j, kk: (i, kk)),
                  pl.BlockSpec((None, bk, bn), lambda i, j, kk: (layer, kk, j)),
                  pl.BlockSpec((bm, bn), lambda i, j, kk: (i, j))],
        out_specs=pl.BlockSpec((bm, bn), lambda i, j, kk: (i, j)),
        out_shape=jax.ShapeDtypeStruct((m, n), F32),
        compiler_params=_params("parallel", "parallel", "arbitrary"),
        name=name,
    )(a, w, res)


NA_BIAS_VARIANTS = NA_WIN_ROWS
NA_BAND_KEYS = NA_WIN_ROWS * GRID_W
RPB_ROWS = 2 * NA_WIN_ROWS - 1
RPB_COLS = 2 * NA_WIN_COLS - 1
NA_ROW_GROUP = 8


def _na_bias_kernel(rpb_ref, o_ref):
    h = pl.program_id(0)
    shape = (GRID_W, 2 * GRID_W)
    lane = lax.broadcasted_iota(jnp.int32, shape, 1)
    cq = lax.broadcasted_iota(jnp.int32, shape, 0)
    ck = lane & (GRID_W - 1)
    upper = lane >= GRID_W
    dcl = jnp.clip(ck - cq, -(NA_WIN_COLS - 1), NA_WIN_COLS - 1) + (NA_WIN_COLS - 1)
    c0 = jnp.clip(cq - NA_WIN_COLS // 2, 0, GRID_W - NA_WIN_COLS)
    col_in = jnp.logical_and(ck >= c0, ck < c0 + NA_WIN_COLS)
    pairs = []
    for d in range(RPB_ROWS - 1):
        acc = jnp.zeros(shape, F32)
        for dc in range(RPB_COLS):
            lo = rpb_ref[(h * RPB_ROWS + d) * RPB_COLS + dc]
            hi = rpb_ref[(h * RPB_ROWS + d + 1) * RPB_COLS + dc]
            acc = jnp.where(dcl == dc, jnp.where(upper, hi, lo), acc)
        pairs.append(jnp.where(col_in, acc, NEG_INF))
    for off in range(NA_BIAS_VARIANTS):
        for jj in range(NA_WIN_ROWS // 2):
            o_ref[0, off, :, jj * 2 * GRID_W:(jj + 1) * 2 * GRID_W] = pairs[off + 2 * jj]


def _na_bias(rpb):
    return pl.pallas_call(
        _na_bias_kernel,
        grid=(NA_HEADS,),
        in_specs=[pl.BlockSpec(memory_space=pltpu.SMEM)],
        out_specs=pl.BlockSpec((1, NA_BIAS_VARIANTS, GRID_W, NA_BAND_KEYS), lambda h: (h, 0, 0, 0)),
        out_shape=jax.ShapeDtypeStruct((NA_HEADS, NA_BIAS_VARIANTS, GRID_W, NA_BAND_KEYS), F32),
        compiler_params=_params("arbitrary"),
        name="na_bias",
    )(rpb.reshape(-1))


def _na_kernel(q_ref, k_ref, v_ref, bias_ref, o_ref, kb_ref, vb_ref, *, seq):
    rows = seq // GRID_W
    cast_rows = 512

    def cast_body(c, carry):
        sl = pl.ds(pl.multiple_of(c * cast_rows, cast_rows), cast_rows)
        kb_ref[sl, :] = k_ref[sl, :].astype(BF16)
        vb_ref[sl, :] = v_ref[sl, :].astype(BF16)
        return carry

    lax.fori_loop(0, seq // cast_rows, cast_body, 0)
    scale = HEAD_DIM ** -0.5

    def group_body(gi, carry):
        qsls, ksls, scores = [], [], []
        for u in range(NA_ROW_GROUP):
            r = gi * NA_ROW_GROUP + u
            r0 = jnp.clip(r - NA_WIN_ROWS // 2, 0, rows - NA_WIN_ROWS)
            off = r0 - r + (NA_WIN_ROWS - 1)
            qsl = pl.ds(pl.multiple_of(r * GRID_W, GRID_W), GRID_W)
            ksl = pl.ds(pl.multiple_of(r0 * GRID_W, GRID_W), NA_BAND_KEYS)
            q = (q_ref[qsl, :] * scale).astype(BF16)
            s = lax.dot_general(q, kb_ref[ksl, :], (((1,), (1,)), ((), ())), preferred_element_type=F32)
            scores.append(s + bias_ref[0, off])
            qsls.append(qsl)
            ksls.append(ksl)
        probs, inv_ls = [], []
        for s in scores:
            p = jnp.exp(s - jnp.max(s, axis=-1, keepdims=True))
            inv_ls.append(1.0 / jnp.sum(p, axis=-1, keepdims=True))
            probs.append(p.astype(BF16))
        for qsl, ksl, p, inv_l in zip(qsls, ksls, probs, inv_ls):
            o = jnp.dot(p, vb_ref[ksl, :], preferred_element_type=F32)
            o_ref[qsl, :] = (o * inv_l).astype(o_ref.dtype)
        return carry

    lax.fori_loop(0, rows // NA_ROW_GROUP, group_body, 0)


def _na_attention(proj, bias, batch, seq):
    t = proj.shape[0]
    qb, kb, vb = NA_Q_COL // HEAD_DIM, NA_K_COL // HEAD_DIM, NA_V_COL // HEAD_DIM
    return pl.pallas_call(
        functools.partial(_na_kernel, seq=seq),
        grid=(batch, NA_HEADS),
        in_specs=[pl.BlockSpec((seq, HEAD_DIM), lambda b, h: (b, qb + h)),
                  pl.BlockSpec((seq, HEAD_DIM), lambda b, h: (b, kb + h)),
                  pl.BlockSpec((seq, HEAD_DIM), lambda b, h: (b, vb + h)),
                  pl.BlockSpec((1, NA_BIAS_VARIANTS, GRID_W, NA_BAND_KEYS), lambda b, h: (h, 0, 0, 0))],
        out_specs=pl.BlockSpec((seq, HEAD_DIM), lambda b, h: (b, h)),
        out_shape=jax.ShapeDtypeStruct((t, NA_WIDTH), BF16),
        scratch_shapes=[pltpu.VMEM((seq, HEAD_DIM), BF16), pltpu.VMEM((seq, HEAD_DIM), BF16)],
        compiler_params=_params("parallel", "parallel"),
        name="na_attention",
    )(proj, proj, proj, bias)


DIFF_K_ROWS = 1024
DIFF_Q_ROWS = 256
DIFF_PREP_COLS = 1024
LOG2_E = 1.4426950408889634


def _rope_tables(seq):
    half = ROT_DIM // 2
    inv_freq = ROPE_THETA ** (-jnp.arange(half, dtype=F32) / half)
    ang = jnp.arange(seq, dtype=F32)[:, None] * inv_freq[None, :]
    cos, sin = jnp.cos(ang), jnp.sin(ang)
    rest = DIFF_QK_DIM - ROT_DIM
    c = jnp.concatenate([cos, cos, jnp.ones((seq, rest), F32)], axis=1)
    sa = jnp.concatenate([-sin, jnp.zeros((seq, DIFF_QK_DIM - half), F32)], axis=1)
    sb = jnp.concatenate([jnp.zeros((seq, half), F32), sin, jnp.zeros((seq, rest), F32)], axis=1)
    return c, sa, sb


def _diff_prep_kernel(q_ref, k_ref, v_ref, c_ref, sa_ref, sb_ref, qt_ref, ko_ref, vt_ref):
    half = ROT_DIM // 2
    c, sa, sb = c_ref[...], sa_ref[...], sb_ref[...]
    qscale = DIFF_QK_DIM ** -0.5 * LOG2_E

    def rope(x):
        return x * c + pltpu.roll(x, DIFF_QK_DIM - half, 1) * sa + pltpu.roll(x, half, 1) * sb

    for g in range(DIFF_PREP_COLS // DIFF_QK_DIM):
        sl = slice(g * DIFF_QK_DIM, (g + 1) * DIFF_QK_DIM)
        qt = (rope(q_ref[:, sl]) * qscale).T.astype(BF16)
        for i in range(DIFF_K_ROWS // DIFF_Q_ROWS):
            qt_ref[0, i, sl, :] = qt[:, i * DIFF_Q_ROWS:(i + 1) * DIFF_Q_ROWS]
        ko_ref[:, sl] = rope(k_ref[:, sl]).astype(BF16)
        vt_ref[0, 0, sl, :] = v_ref[:, sl].T.astype(BF16)


def _diff_prep(proj, batch, seq):
    t = proj.shape[0]
    rb, cb, bq = DIFF_K_ROWS, DIFF_PREP_COLS, DIFF_Q_ROWS
    nsb = seq // rb
    c, sa, sb = _rope_tables(seq)
    qc, kc, vc = DIFF_Q_COL // cb, DIFF_K_COL // cb, DIFF_V_COL // cb
    tab_spec = pl.BlockSpec((rb, DIFF_QK_DIM), lambda i, j: (i % nsb, 0))
    return pl.pallas_call(
        _diff_prep_kernel,
        grid=(t // rb, DIFF_WIDTH // cb),
        in_specs=[pl.BlockSpec((rb, cb), lambda i, j: (i, qc + j)),
                  pl.BlockSpec((rb, cb), lambda i, j: (i, kc + j)),
                  pl.BlockSpec((rb, cb), lambda i, j: (i, vc + j)),
                  tab_spec, tab_spec, tab_spec],
        out_specs=[pl.BlockSpec((1, rb // bq, cb, bq), lambda i, j: (i // nsb, i % nsb, j, 0)),
                   pl.BlockSpec((rb, cb), lambda i, j: (i, j)),
                   pl.BlockSpec((1, 1, cb, rb), lambda i, j: (i // nsb, i % nsb, j, 0))],
        out_shape=[jax.ShapeDtypeStruct((batch, seq // bq, DIFF_WIDTH, bq), BF16),
                   jax.ShapeDtypeStruct((t, DIFF_WIDTH), BF16),
                   jax.ShapeDtypeStruct((batch, nsb, DIFF_WIDTH, rb), BF16)],
        compiler_params=_params("parallel", "parallel"),
        name="diff_prep",
    )(proj, proj, proj, c, sa, sb)


def _diff_kernel(lam_ref, qt_ref, k_ref, vt_ref, lq1_ref, lk1_ref, lq2_ref, lk2_ref, g_ref, o_ref, acc_ref, s_ref,
                 *, seq):
    bk = DIFF_K_ROWS
    bq = DIFF_Q_ROWS
    n_chunks = seq // bk
    assert n_chunks % 2 == 0
    acc_ref[...] = jnp.zeros(acc_ref.shape, F32)
    csl = [slice(c * DIFF_QK_DIM, (c + 1) * DIFF_QK_DIM) for c in range(2)]

    def scores(j, slot):
        ksl = pl.ds(pl.multiple_of(j * bk, bk), bk)
        for c in range(2):
            s_ref[slot, c] = jnp.dot(k_ref[ksl, csl[c]], qt_ref[0, 0, csl[c], :], preferred_element_type=F32)

    def consume(j, slot, carry):
        ms, ls = carry
        vt = vt_ref[0, j]
        new_ms, new_ls = [], []
        for c in range(2):
            st = s_ref[slot, c]
            m_new = jnp.maximum(ms[c], jnp.max(st, axis=0, keepdims=True))
            alpha = jnp.exp2(ms[c] - m_new)
            p = jnp.exp2(st - m_new)
            new_ls.append(alpha * ls[c] + jnp.sum(p, axis=0, keepdims=True))
            new_ms.append(m_new)
            acc_ref[c] = alpha * acc_ref[c] + jnp.dot(vt, p.astype(BF16), preferred_element_type=F32)
        return tuple(new_ms), tuple(new_ls)

    def body(i, carry):
        j = 2 * i
        scores(j + 1, 1)
        carry = consume(j, 0, carry)
        scores(j + 2, 0)
        return consume(j + 1, 1, carry)

    m0 = jnp.full((1, bq), NEG_INF, F32)
    l0 = jnp.zeros((1, bq), F32)
    scores(0, 0)
    carry = lax.fori_loop(0, n_chunks // 2 - 1, body, ((m0, m0), (l0, l0)))
    scores(n_chunks - 1, 1)
    carry = consume(n_chunks - 2, 0, carry)
    _, ls = consume(n_chunks - 1, 1, carry)
    lam_init = lam_ref[0]
    lam = (jnp.exp(jnp.sum(lq1_ref[...] * lk1_ref[...], axis=-1, keepdims=True))
           - jnp.exp(jnp.sum(lq2_ref[...] * lk2_ref[...], axis=-1, keepdims=True)) + lam_init)
    o = acc_ref[0] * (1.0 / ls[0]) - lam * (acc_ref[1] * (1.0 / ls[1]))
    ms = jnp.mean(o * o, axis=0, keepdims=True)
    y = (o * lax.rsqrt(ms + NORM_EPS)).T * g_ref[...]
    o_ref[...] = (y * lam_ref[1]).astype(o_ref.dtype)


def _diff_attention(qt, k, vt, lam_consts, lq1, lk1, lq2, lk2, subln_g, batch, seq):
    t = k.shape[0]
    bq = DIFF_Q_ROWS
    nqb = seq // bq
    nkb = seq // DIFF_K_ROWS
    vec = lambda n: pl.BlockSpec((1, n), lambda b, h, i: (0, 0))
    return pl.pallas_call(
        functools.partial(_diff_kernel, seq=seq),
        grid=(batch, DIFF_HEADS, nqb),
        in_specs=[pl.BlockSpec(memory_space=pltpu.SMEM),
                  pl.BlockSpec((1, 1, DIFF_V_DIM, bq), lambda b, h, i: (b, i, h, 0)),
                  pl.BlockSpec((seq, DIFF_V_DIM), lambda b, h, i: (b, h)),
                  pl.BlockSpec((1, nkb, DIFF_V_DIM, DIFF_K_ROWS), lambda b, h, i: (b, 0, h, 0)),
                  vec(DIFF_QK_DIM), vec(DIFF_QK_DIM), vec(DIFF_QK_DIM), vec(DIFF_QK_DIM), vec(DIFF_V_DIM)],
        out_specs=pl.BlockSpec((bq, DIFF_V_DIM), lambda b, h, i: (b * nqb + i, h)),
        out_shape=jax.ShapeDtypeStruct((t, DIFF_WIDTH), BF16),
        scratch_shapes=[pltpu.VMEM((2, DIFF_V_DIM, bq), F32),
                        pltpu.VMEM((2, 2, DIFF_K_ROWS, bq), F32)],
        compiler_params=_params("parallel", "parallel", "arbitrary"),
        name="diff_attention",
    )(lam_consts, qt, k, vt, lq1.reshape(1, -1), lk1.reshape(1, -1), lq2.reshape(1, -1), lk2.reshape(1, -1),
      subln_g.reshape(1, -1))


CONV_CHUNK = 512


def _conv_kernel(u_ref, b_ref, c_ref, w_ref, o_ref, *, seq):
    n_chunks = seq // CONV_CHUNK
    w0, w1, w2 = w_ref[0:1, :], w_ref[1:2, :], w_ref[2:3, :]
    row = lax.broadcasted_iota(jnp.int32, (CONV_CHUNK, LANES), 0)

    def body(ci, carry):
        t0 = pl.multiple_of(ci * CONV_CHUNK, CONV_CHUNK)
        sl = pl.ds(t0, CONV_CHUNK)
        z = c_ref[sl, :] * u_ref[sl, :]
        psl = pl.ds(pl.multiple_of(jnp.maximum(t0 - 8, 0), 8), 8)
        nsl = pl.ds(pl.multiple_of(jnp.minimum(t0 + CONV_CHUNK, seq - 8), 8), 8)
        zp = (c_ref[psl, :] * u_ref[psl, :])[7:8, :]
        zn = (c_ref[nsl, :] * u_ref[nsl, :])[0:1, :]
        zp = jnp.where(ci > 0, zp, 0.0)
        zn = jnp.where(ci < n_chunks - 1, zn, 0.0)
        z_prev = jnp.where(row == 0, zp, pltpu.roll(z, 1, 0))
        z_next = jnp.where(row == CONV_CHUNK - 1, zn, pltpu.roll(z, CONV_CHUNK - 1, 0))
        y = w0 * z_prev + w1 * z + w2 * z_next
        o_ref[sl, :] = (b_ref[sl, :] * y).astype(o_ref.dtype)
        return carry

    lax.fori_loop(0, n_chunks, body, 0)


def _short_conv(proj, conv_w, batch, seq):
    t = proj.shape[0]
    ub, bb, cb = CONV_U_COL // LANES, CONV_B_COL // LANES, CONV_C_COL // LANES
    return pl.pallas_call(
        functools.partial(_conv_kernel, seq=seq),
        grid=(batch, CONV_WIDTH // LANES),
        in_specs=[pl.BlockSpec((seq, LANES), lambda b, j: (b, ub + j)),
                  pl.BlockSpec((seq, LANES), lambda b, j: (b, bb + j)),
                  pl.BlockSpec((seq, LANES), lambda b, j: (b, cb + j)),
                  pl.BlockSpec((3, LANES), lambda b, j: (0, j))],
        out_specs=pl.BlockSpec((seq, LANES), lambda b, j: (b, j)),
        out_shape=jax.ShapeDtypeStruct((t, CONV_WIDTH), BF16),
        compiler_params=_params("parallel", "parallel"),
        name="short_conv",
    )(proj, proj, proj, conv_w)


def _trunk_layer(x, layer_idx, batch, seq, norm1_g, w_in, na_bias, lq1, lk1, lq2, lk2, subln_g, conv_w,
                 w_out, norm2_g, w_up, w_down):
    h = _rmsnorm(x, norm1_g, BF16)
    proj = _matmul([h], w_in, layer_idx, F32, name="in_proj")
    na_out = _na_attention(proj, na_bias, batch, seq)
    lam_init = 0.8 - 0.6 * math.exp(-0.3 * layer_idx)
    lam_consts = jnp.array([lam_init, 1.0 - lam_init], F32)
    dq, dkt, dv = _diff_prep(proj, batch, seq)
    diff_out = _diff_attention(dq, dkt, dv, lam_consts, lq1, lk1, lq2, lk2, subln_g, batch, seq)
    conv_out = _short_conv(proj, conv_w, batch, seq)
    x = _matmul([na_out, diff_out, conv_out], w_out, layer_idx, F32, res=x, name="out_proj")
    h2 = _rmsnorm(x, norm2_g, BF16)
    act = _matmul([h2], w_up, layer_idx, BF16, relu2=True, name="mlp_up")
    return _matmul_kgrid(act, w_down, layer_idx, x, name="mlp_down")


def _run_trunk(x, layers, final_norm_g):
    batch, seq, d = x.shape
    x = x.reshape(batch * seq, d)
    for i, lw in enumerate(layers):
        x = _trunk_layer(x, i, batch, seq, *lw)
    return _rmsnorm(x, final_norm_g, F32).reshape(batch, seq, d)


def kernel(x_prompt, x_sample, norm1_g, w_in, na_rpb, diff_lq1, diff_lk1, diff_lq2, diff_lk2, diff_subln_g,
           conv_w, w_out, norm2_g, w_up, w_down, final_norm_g):
    w_in_b, w_out_b, w_up_b, w_down_b = (w.astype(BF16) for w in (w_in, w_out, w_up, w_down))
    layers = []
    for i in range(DEPTH):
        layers.append((norm1_g[i], w_in_b, _na_bias(na_rpb[i]), diff_lq1[i], diff_lk1[i],
                       diff_lq2[i], diff_lk2[i], diff_subln_g[i], conv_w[i], w_out_b,
                       norm2_g[i], w_up_b, w_down_b))
    return (_run_trunk(x_prompt, layers, final_norm_g), _run_trunk(x_sample, layers, final_norm_g))
```

```python
import functools
import math

import jax
import jax.numpy as jnp
from jax import lax
from jax.experimental import pallas as pl
from jax.experimental.pallas import tpu as pltpu

F32 = jnp.float32
BF16 = jnp.bfloat16

D_MODEL = 4096
DEPTH = 4
GRID_W = 64
HEAD_DIM = 128
NA_HEADS = D_MODEL // 512
NA_WIDTH = NA_HEADS * HEAD_DIM
NA_WIN_ROWS = 8
NA_WIN_COLS = 16
DIFF_HEADS = D_MODEL // 512
DIFF_QK_DIM = HEAD_DIM
DIFF_V_DIM = 2 * DIFF_QK_DIM
DIFF_WIDTH = DIFF_HEADS * DIFF_V_DIM
CONV_WIDTH = D_MODEL - NA_WIDTH - DIFF_WIDTH
IN_WIDTH = 3 * NA_WIDTH + 3 * DIFF_WIDTH + 3 * CONV_WIDTH
D_FF = 4 * D_MODEL
ROPE_THETA = 500000.0
ROT_DIM = DIFF_QK_DIM // 4
NORM_EPS = 1e-5
NEG_INF = -1e30
LOG2_E = 1.4426950408889634

NA_COL = 0
DIFF_Q_COL = 3 * NA_WIDTH
DIFF_K_COL = DIFF_Q_COL + DIFF_WIDTH
DIFF_V_COL = DIFF_K_COL + DIFF_WIDTH
CONV_COL = DIFF_V_COL + DIFF_WIDTH

V7X_VMEM_BYTES = 64 * 1024 * 1024
V7X_VMEM_LIMIT_BYTES = V7X_VMEM_BYTES - 8 * 1024 * 1024
V7X_VMEM_LIMIT_WIDE_BYTES = V7X_VMEM_BYTES - 2 * 1024 * 1024
LANES = 128
MM_TILE = 1024


def _params(*sem, vmem_limit_bytes=V7X_VMEM_LIMIT_BYTES):
    return pltpu.CompilerParams(dimension_semantics=sem, vmem_limit_bytes=vmem_limit_bytes)


def _rmsnorm_kernel(x_ref, g_ref, o_ref):
    x = x_ref[...]
    ms = jnp.mean(x * x, axis=-1, keepdims=True)
    o_ref[...] = ((x * lax.rsqrt(ms + NORM_EPS)) * g_ref[...]).astype(o_ref.dtype)


def _rmsnorm(x, g, out_dtype, block_rows=256):
    t, d = x.shape
    return pl.pallas_call(
        _rmsnorm_kernel,
        grid=(t // block_rows,),
        in_specs=[pl.BlockSpec((block_rows, d), lambda i: (i, 0)),
                  pl.BlockSpec((1, d), lambda i: (0, 0))],
        out_specs=pl.BlockSpec((block_rows, d), lambda i: (i, 0)),
        out_shape=jax.ShapeDtypeStruct((t, d), out_dtype),
        compiler_params=_params("parallel"),
        name="rmsnorm",
    )(x, g.reshape(1, d))


def _norm_entry_kernel(x_ref, g_ref, xg_ref, ssq_ref):
    x = x_ref[...]
    xg_ref[...] = (x * g_ref[...]).astype(BF16)
    ssq_ref[...] = jnp.sum(x * x, axis=-1, keepdims=True)


def _norm_entry(x, g, block_rows=256):
    t, d = x.shape
    return pl.pallas_call(
        _norm_entry_kernel,
        grid=(t // block_rows,),
        in_specs=[pl.BlockSpec((block_rows, d), lambda i: (i, 0)),
                  pl.BlockSpec((1, d), lambda i: (0, 0))],
        out_specs=[pl.BlockSpec((block_rows, d), lambda i: (i, 0)),
                   pl.BlockSpec((block_rows, 1), lambda i: (i, 0))],
        out_shape=[jax.ShapeDtypeStruct((t, d), BF16), jax.ShapeDtypeStruct((t, 1), F32)],
        compiler_params=_params("parallel"),
        name="norm_entry",
    )(x, g.reshape(1, d))


def _row_rscale(ssq_ref):
    return lax.rsqrt(ssq_ref[...] * (1.0 / D_MODEL) + NORM_EPS)


def _rope_lanes(x, c, sa, sb):
    half = ROT_DIM // 2
    return x * c + pltpu.roll(x, DIFF_QK_DIM - half, 1) * sa + pltpu.roll(x, half, 1) * sb


def _proj_kernel(*refs, mode):
    xg_ref, w_ref, ssq_ref = refs[:3]
    o_ref = refs[-1]
    acc = jnp.dot(xg_ref[...], w_ref[...], preferred_element_type=F32) * _row_rscale(ssq_ref)
    groups = [slice(g * LANES, (g + 1) * LANES) for g in range(acc.shape[1] // LANES)]
    if mode == "f32":
        o_ref[...] = acc
    elif mode == "relu2":
        r = jnp.maximum(acc, 0.0)
        o_ref[...] = (r * r).astype(o_ref.dtype)
    elif mode == "colscale":
        o_ref[...] = (acc * refs[3][...]).astype(o_ref.dtype)
    elif mode == "rope":
        tab_ref = refs[3]
        c, sa, sb = tab_ref[0], tab_ref[1], tab_ref[2]
        for sl in groups:
            o_ref[:, sl] = _rope_lanes(acc[:, sl], c, sa, sb).astype(o_ref.dtype)
    elif mode == "transpose":
        for sl in groups:
            o_ref[0, 0, sl, :] = acc[:, sl].T.astype(o_ref.dtype)
    else:
        raise ValueError(mode)


def _project(xg, ssq, w, layer, col0, n, mode, out_dtype, extra=None, extra_spec=None, batch=None, name="proj"):
    m, k = xg.shape
    bm = bn = MM_TILE
    cb0 = col0 // bn
    in_specs = [pl.BlockSpec((bm, k), lambda i, j: (i, 0)),
                pl.BlockSpec((None, k, bn), lambda i, j: (layer, 0, cb0 + j)),
                pl.BlockSpec((bm, 1), lambda i, j: (i, 0))]
    args = [xg, w, ssq]
    if extra is not None:
        in_specs.append(extra_spec)
        args.append(extra)
    if mode == "transpose":
        nsb = m // batch // bm
        out_spec = pl.BlockSpec((1, 1, bn, bm), lambda i, j: (i // nsb, i % nsb, j, 0))
        out_shape = jax.ShapeDtypeStruct((batch, nsb, n, bm), out_dtype)
    else:
        out_spec = pl.BlockSpec((bm, bn), lambda i, j: (i, j))
        out_shape = jax.ShapeDtypeStruct((m, n), out_dtype)
    return pl.pallas_call(
        functools.partial(_proj_kernel, mode=mode),
        grid=(m // bm, n // bn),
        in_specs=in_specs,
        out_specs=out_spec,
        out_shape=out_shape,
        compiler_params=_params("parallel", "arbitrary"),
        name=name,
    )(*args)


def _emit_norm_inputs(x, g_ref, xg_ref, ssq_ref, first):
    xg_ref[...] = (x * g_ref[...]).astype(BF16)
    part = jnp.sum(x * x, axis=-1, keepdims=True)

    @pl.when(first)
    def _():
        ssq_ref[...] = part

    @pl.when(jnp.logical_not(first))
    def _():
        ssq_ref[...] += part


def _out_proj_kernel(a1_ref, a2_ref, a3_ref, w_ref, res_ref, g_ref, x_ref, xg_ref, ssq_ref, *, k_sizes):
    acc = None
    k0 = 0
    for a_ref, ks in zip((a1_ref, a2_ref, a3_ref), k_sizes):
        d = jnp.dot(a_ref[...], w_ref[k0:k0 + ks, :], preferred_element_type=F32)
        acc = d if acc is None else acc + d
        k0 += ks
    x = acc + res_ref[...]
    x_ref[...] = x
    _emit_norm_inputs(x, g_ref, xg_ref, ssq_ref, pl.program_id(1) == 0)


def _out_proj(a_list, w, layer, res, g):
    m = a_list[0].shape[0]
    k_sizes = tuple(a.shape[1] for a in a_list)
    _, k, n = w.shape
    assert sum(k_sizes) == k
    bm = bn = MM_TILE
    tile = pl.BlockSpec((bm, bn), lambda i, j: (i, j))
    return pl.pallas_call(
        functools.partial(_out_proj_kernel, k_sizes=k_sizes),
        grid=(m // bm, n // bn),
        in_specs=[pl.BlockSpec((bm, ks), lambda i, j: (i, 0)) for ks in k_sizes]
        + [pl.BlockSpec((None, k, bn), lambda i, j: (layer, 0, j)), tile, pl.BlockSpec((1, bn), lambda i, j: (0, j))],
        out_specs=[tile, tile, pl.BlockSpec((bm, 1), lambda i, j: (i, 0))],
        out_shape=[jax.ShapeDtypeStruct((m, n), F32), jax.ShapeDtypeStruct((m, n), BF16),
                   jax.ShapeDtypeStruct((m, 1), F32)],
        compiler_params=_params("parallel", "arbitrary", vmem_limit_bytes=V7X_VMEM_LIMIT_WIDE_BYTES),
        name="out_proj",
    )(*a_list, w, res, g.reshape(1, n))


def _mlp_down_kernel(*refs, emit_norm):
    if emit_norm:
        a_ref, w_ref, res_ref, g_ref, o_ref, xg_ref, ssq_ref = refs
    else:
        a_ref, w_ref, res_ref, o_ref = refs
    kk = pl.program_id(2)

    @pl.when(kk == 0)
    def _():
        o_ref[...] = res_ref[...]

    o_ref[...] += jnp.dot(a_ref[...], w_ref[...], preferred_element_type=F32)

    if emit_norm:
        @pl.when(kk == pl.num_programs(2) - 1)
        def _():
            _emit_norm_inputs(o_ref[...], g_ref, xg_ref, ssq_ref, pl.program_id(1) == 0)


def _mlp_down(a, w, layer, res, g=None, bk=4096):
    m, k = a.shape
    n = w.shape[2]
    bm = bn = MM_TILE
    emit_norm = g is not None
    tile = pl.BlockSpec((bm, bn), lambda i, j, kk: (i, j))
    in_specs = [pl.BlockSpec((bm, bk), lambda i, j, kk: (i, kk)),
                pl.BlockSpec((None, bk, bn), lambda i, j, kk: (layer, kk, j)),
                tile]
    args = [a, w, res]
    out_specs = [tile]
    out_shape = [jax.ShapeDtypeStruct((m, n), F32)]
    if emit_norm:
        in_specs.append(pl.BlockSpec((1, bn), lambda i, j, kk: (0, j)))
        args.append(g.reshape(1, n))
        out_specs += [tile, pl.BlockSpec((bm, 1), lambda i, j, kk: (i, 0))]
        out_shape += [jax.ShapeDtypeStruct((m, n), BF16), jax.ShapeDtypeStruct((m, 1), F32)]
    return pl.pallas_call(
        functools.partial(_mlp_down_kernel, emit_norm=emit_norm),
        grid=(m // bm, n // bn, k // bk),
        in_specs=in_specs,
        out_specs=out_specs,
        out_shape=out_shape,
        compiler_params=_params("parallel", "arbitrary", "arbitrary", vmem_limit_bytes=V7X_VMEM_LIMIT_WIDE_BYTES),
        name="mlp_down",
    )(*args)


NA_BIAS_VARIANTS = NA_WIN_ROWS
NA_BAND_KEYS = NA_WIN_ROWS * GRID_W
RPB_ROWS = 2 * NA_WIN_ROWS - 1
RPB_COLS = 2 * NA_WIN_COLS - 1
NA_ROW_GROUP = 16


def _na_bias_kernel(rpb_ref, o_ref):
    h = pl.program_id(0)
    shape = (GRID_W, 2 * GRID_W)
    lane = lax.broadcasted_iota(jnp.int32, shape, 1)
    cq = lax.broadcasted_iota(jnp.int32, shape, 0)
    ck = lane & (GRID_W - 1)
    upper = lane >= GRID_W
    dcl = jnp.clip(ck - cq, -(NA_WIN_COLS - 1), NA_WIN_COLS - 1) + (NA_WIN_COLS - 1)
    c0 = jnp.clip(cq - NA_WIN_COLS // 2, 0, GRID_W - NA_WIN_COLS)
    col_in = jnp.logical_and(ck >= c0, ck < c0 + NA_WIN_COLS)
    pairs = []
    for d in range(RPB_ROWS - 1):
        acc = jnp.zeros(shape, F32)
        for dc in range(RPB_COLS):
            lo = rpb_ref[(h * RPB_ROWS + d) * RPB_COLS + dc]
            hi = rpb_ref[(h * RPB_ROWS + d + 1) * RPB_COLS + dc]
            acc = jnp.where(dcl == dc, jnp.where(upper, hi, lo), acc)
        pairs.append(jnp.where(col_in, acc, NEG_INF))
    for off in range(NA_BIAS_VARIANTS):
        for jj in range(NA_WIN_ROWS // 2):
            o_ref[0, off, :, jj * 2 * GRID_W:(jj + 1) * 2 * GRID_W] = pairs[off + 2 * jj]


def _na_bias(rpb):
    return pl.pallas_call(
        _na_bias_kernel,
        grid=(NA_HEADS,),
        in_specs=[pl.BlockSpec(memory_space=pltpu.SMEM)],
        out_specs=pl.BlockSpec((1, NA_BIAS_VARIANTS, GRID_W, NA_BAND_KEYS), lambda h: (h, 0, 0, 0)),
        out_shape=jax.ShapeDtypeStruct((NA_HEADS, NA_BIAS_VARIANTS, GRID_W, NA_BAND_KEYS), F32),
        compiler_params=_params("arbitrary"),
        name="na_bias",
    )(rpb.reshape(-1))


def _na_kernel(q_ref, k_ref, v_ref, bias_ref, o_ref, *, seq):
    rows = seq // GRID_W

    def group_body(gi, carry):
        qsls, ksls, scores = [], [], []
        for u in range(NA_ROW_GROUP):
            r = gi * NA_ROW_GROUP + u
            r0 = jnp.clip(r - NA_WIN_ROWS // 2, 0, rows - NA_WIN_ROWS)
            off = r0 - r + (NA_WIN_ROWS - 1)
            qsl = pl.ds(pl.multiple_of(r * GRID_W, GRID_W), GRID_W)
            ksl = pl.ds(pl.multiple_of(r0 * GRID_W, GRID_W), NA_BAND_KEYS)
            s = lax.dot_general(q_ref[qsl, :], k_ref[ksl, :], (((1,), (1,)), ((), ())),
                                preferred_element_type=F32)
            scores.append(s + bias_ref[0, off])
            qsls.append(qsl)
            ksls.append(ksl)
        probs, inv_ls = [], []
        for s in scores:
            p = jnp.exp(s - jnp.max(s, axis=-1, keepdims=True))
            inv_ls.append(1.0 / jnp.sum(p, axis=-1, keepdims=True))
            probs.append(p.astype(BF16))
        for qsl, ksl, p, inv_l in zip(qsls, ksls, probs, inv_ls):
            o = jnp.dot(p, v_ref[ksl, :], preferred_element_type=F32)
            o_ref[qsl, :] = (o * inv_l).astype(o_ref.dtype)
        return carry

    lax.fori_loop(0, rows // NA_ROW_GROUP, group_body, 0)


def _na_attention(qkv, bias, batch, seq):
    t = qkv.shape[0]
    return pl.pallas_call(
        functools.partial(_na_kernel, seq=seq),
        grid=(batch, NA_HEADS),
        in_specs=[pl.BlockSpec((seq, HEAD_DIM), lambda b, h: (b, h)),
                  pl.BlockSpec((seq, HEAD_DIM), lambda b, h: (b, NA_HEADS + h)),
                  pl.BlockSpec((seq, HEAD_DIM), lambda b, h: (b, 2 * NA_HEADS + h)),
                  pl.BlockSpec((1, NA_BIAS_VARIANTS, GRID_W, NA_BAND_KEYS), lambda b, h: (h, 0, 0, 0))],
        out_specs=pl.BlockSpec((seq, HEAD_DIM), lambda b, h: (b, h)),
        out_shape=jax.ShapeDtypeStruct((t, NA_WIDTH), BF16),
        compiler_params=_params("parallel", "parallel"),
        name="na_attention",
    )(qkv, qkv, qkv, bias)


DIFF_K_ROWS = MM_TILE
DIFF_Q_ROWS = 256


def _rope_tables(seq):
    half = ROT_DIM // 2
    inv_freq = ROPE_THETA ** (-jnp.arange(half, dtype=F32) / half)
    ang = jnp.arange(seq, dtype=F32)[:, None] * inv_freq[None, :]
    cos, sin = jnp.cos(ang), jnp.sin(ang)
    rest = DIFF_QK_DIM - ROT_DIM
    c = jnp.concatenate([cos, cos, jnp.ones((seq, rest), F32)], axis=1)
    sa = jnp.concatenate([-sin, jnp.zeros((seq, DIFF_QK_DIM - half), F32)], axis=1)
    sb = jnp.concatenate([jnp.zeros((seq, half), F32), sin, jnp.zeros((seq, rest), F32)], axis=1)
    k_tabs = jnp.stack([c, sa, sb])
    return jnp.stack([k_tabs * (DIFF_QK_DIM ** -0.5 * LOG2_E), k_tabs])


def _diff_kernel(lam_ref, q_ref, k_ref, vt_ref, lq1_ref, lk1_ref, lq2_ref, lk2_ref, g_ref, o_ref, acc_ref, s_ref,
                 *, seq):
    bk = DIFF_K_ROWS
    bq = DIFF_Q_ROWS
    n_chunks = seq // bk
    n_q = seq // bq
    assert n_chunks % 2 == 0
    csl = [slice(c * DIFF_QK_DIM, (c + 1) * DIFF_QK_DIM) for c in range(2)]
    lam_init = lam_ref[0]
    lam = (jnp.exp(jnp.sum(lq1_ref[...] * lk1_ref[...], axis=-1, keepdims=True))
           - jnp.exp(jnp.sum(lq2_ref[...] * lk2_ref[...], axis=-1, keepdims=True)) + lam_init)

    def scores(qi, j, slot):
        qsl = pl.ds(pl.multiple_of(qi * bq, bq), bq)
        for c in range(2):
            s_ref[slot, c] = lax.dot_general(k_ref[j * bk:(j + 1) * bk, csl[c]], q_ref[qsl, csl[c]],
                                             (((1,), (1,)), ((), ())), preferred_element_type=F32)

    def consume(j, slot, carry):
        vt = vt_ref[0, j]
        new_ms, new_ls = [], []
        for c in range(2):
            st = s_ref[slot, c]
            m_blk = jnp.max(st, axis=0, keepdims=True)
            if j == 0:
                m_new = m_blk
                p = jnp.exp2(st - m_new)
                new_ls.append(jnp.sum(p, axis=0, keepdims=True))
                acc_ref[c] = jnp.dot(vt, p.astype(BF16), preferred_element_type=F32)
            else:
                ms, ls = carry
                m_new = jnp.maximum(ms[c], m_blk)
                alpha = jnp.exp2(ms[c] - m_new)
                p = jnp.exp2(st - m_new)
                new_ls.append(alpha * ls[c] + jnp.sum(p, axis=0, keepdims=True))
                acc_ref[c] = alpha * acc_ref[c] + jnp.dot(vt, p.astype(BF16), preferred_element_type=F32)
            new_ms.append(m_new)
        return tuple(new_ms), tuple(new_ls)

    def q_block(qi, _):
        carry = None
        for j in range(n_chunks):
            slot = j % 2
            if j + 1 < n_chunks:
                scores(qi, j + 1, 1 - slot)
            else:
                scores(jnp.minimum(qi + 1, n_q - 1), 0, 1 - slot)
            carry = consume(j, slot, carry)
        _, ls = carry
        o = acc_ref[0] * (1.0 / ls[0]) - lam * (acc_ref[1] * (1.0 / ls[1]))
        ms = jnp.mean(o * o, axis=0, keepdims=True)
        y = (o * lax.rsqrt(ms + NORM_EPS)).T * g_ref[...]
        o_ref[pl.ds(pl.multiple_of(qi * bq, bq), bq), :] = (y * lam_ref[1]).astype(o_ref.dtype)
        return 0

    scores(0, 0, 0)
    lax.fori_loop(0, n_q, q_block, 0)


def _diff_attention(qk, vt, lam_consts, lq1, lk1, lq2, lk2, subln_g, batch, seq):
    t = qk.shape[0]
    bq = DIFF_Q_ROWS
    nkb = seq // DIFF_K_ROWS
    vec = lambda n: pl.BlockSpec((1, n), lambda b, h: (0, 0))
    return pl.pallas_call(
        functools.partial(_diff_kernel, seq=seq),
        grid=(batch, DIFF_HEADS),
        in_specs=[pl.BlockSpec(memory_space=pltpu.SMEM),
                  pl.BlockSpec((seq, DIFF_V_DIM), lambda b, h: (b, h)),
                  pl.BlockSpec((seq, DIFF_V_DIM), lambda b, h: (b, DIFF_HEADS + h)),
                  pl.BlockSpec((1, nkb, DIFF_V_DIM, DIFF_K_ROWS), lambda b, h: (b, 0, h, 0)),
                  vec(DIFF_QK_DIM), vec(DIFF_QK_DIM), vec(DIFF_QK_DIM), vec(DIFF_QK_DIM), vec(DIFF_V_DIM)],
        out_specs=pl.BlockSpec((seq, DIFF_V_DIM), lambda b, h: (b, h)),
        out_shape=jax.ShapeDtypeStruct((t, DIFF_WIDTH), BF16),
        scratch_shapes=[pltpu.VMEM((2, DIFF_V_DIM, bq), F32),
                        pltpu.VMEM((2, 2, DIFF_K_ROWS, bq), F32)],
        compiler_params=_params("parallel", "parallel"),
        name="diff_attention",
    )(lam_consts, qk, qk, vt, lq1.reshape(1, -1), lk1.reshape(1, -1), lq2.reshape(1, -1), lk2.reshape(1, -1),
      subln_g.reshape(1, -1))


CONV_CHUNK = 512


def _conv_kernel(u_ref, b_ref, c_ref, w_ref, o_ref, *, seq):
    n_chunks = seq // CONV_CHUNK
    w0, w1, w2 = w_ref[0:1, :], w_ref[1:2, :], w_ref[2:3, :]
    row = lax.broadcasted_iota(jnp.int32, (CONV_CHUNK, LANES), 0)

    def body(ci, carry):
        t0 = pl.multiple_of(ci * CONV_CHUNK, CONV_CHUNK)
        sl = pl.ds(t0, CONV_CHUNK)
        z = c_ref[sl, :] * u_ref[sl, :]
        psl = pl.ds(pl.multiple_of(jnp.maximum(t0 - 8, 0), 8), 8)
        nsl = pl.ds(pl.multiple_of(jnp.minimum(t0 + CONV_CHUNK, seq - 8), 8), 8)
        zp = (c_ref[psl, :] * u_ref[psl, :])[7:8, :]
        zn = (c_ref[nsl, :] * u_ref[nsl, :])[0:1, :]
        zp = jnp.where(ci > 0, zp, 0.0)
        zn = jnp.where(ci < n_chunks - 1, zn, 0.0)
        z_prev = jnp.where(row == 0, zp, pltpu.roll(z, 1, 0))
        z_next = jnp.where(row == CONV_CHUNK - 1, zn, pltpu.roll(z, CONV_CHUNK - 1, 0))
        y = w0 * z_prev + w1 * z + w2 * z_next
        o_ref[sl, :] = (b_ref[sl, :] * y).astype(o_ref.dtype)
        return carry

    lax.fori_loop(0, n_chunks, body, 0)


def _short_conv(ubc, conv_w, batch, seq):
    t = ubc.shape[0]
    nb = CONV_WIDTH // LANES
    return pl.pallas_call(
        functools.partial(_conv_kernel, seq=seq),
        grid=(batch, nb),
        in_specs=[pl.BlockSpec((seq, LANES), lambda b, j: (b, j)),
                  pl.BlockSpec((seq, LANES), lambda b, j: (b, nb + j)),
                  pl.BlockSpec((seq, LANES), lambda b, j: (b, 2 * nb + j)),
                  pl.BlockSpec((3, LANES), lambda b, j: (0, j))],
        out_specs=pl.BlockSpec((seq, LANES), lambda b, j: (b, j)),
        out_shape=jax.ShapeDtypeStruct((t, CONV_WIDTH), BF16),
        compiler_params=_params("parallel", "parallel"),
        name="short_conv",
    )(ubc, ubc, ubc, conv_w)


def _na_col_scale():
    return jnp.concatenate([jnp.full((1, NA_WIDTH), HEAD_DIM ** -0.5, F32), jnp.ones((1, 2 * NA_WIDTH), F32)], axis=1)


def _trunk_layer(x, xg, ssq, layer_idx, batch, seq, rope_tabs, next_norm1_g, w_in, na_bias, lq1, lk1, lq2, lk2,
                 subln_g, conv_w, w_out, norm2_g, w_up, w_down):
    nsb = seq // MM_TILE
    na_qkv = _project(xg, ssq, w_in, layer_idx, NA_COL, 3 * NA_WIDTH, "colscale", BF16, extra=_na_col_scale(),
                      extra_spec=pl.BlockSpec((1, MM_TILE), lambda i, j: (0, j)), name="in_proj_na")
    heads_per_tile = DIFF_WIDTH // MM_TILE
    diff_qk = _project(xg, ssq, w_in, layer_idx, DIFF_Q_COL, 2 * DIFF_WIDTH, "rope", BF16, extra=rope_tabs,
                       extra_spec=pl.BlockSpec((None, 3, MM_TILE, DIFF_QK_DIM),
                                               lambda i, j: (j // heads_per_tile, 0, i % nsb, 0)),
                       name="in_proj_qk")
    diff_vt = _project(xg, ssq, w_in, layer_idx, DIFF_V_COL, DIFF_WIDTH, "transpose", BF16, batch=batch,
                       name="in_proj_vt")
    conv_ubc = _project(xg, ssq, w_in, layer_idx, CONV_COL, 3 * CONV_WIDTH, "f32", F32, name="in_proj_conv")

    na_out = _na_attention(na_qkv, na_bias, batch, seq)
    lam_init = 0.8 - 0.6 * math.exp(-0.3 * layer_idx)
    lam_consts = jnp.array([lam_init, 1.0 - lam_init], F32)
    diff_out = _diff_attention(diff_qk, diff_vt, lam_consts, lq1, lk1, lq2, lk2, subln_g, batch, seq)
    conv_out = _short_conv(conv_ubc, conv_w, batch, seq)

    x, xg2, ssq2 = _out_proj([na_out, diff_out, conv_out], w_out, layer_idx, x, norm2_g)
    act = _project(xg2, ssq2, w_up, layer_idx, 0, D_FF, "relu2", BF16, name="mlp_up")
    if next_norm1_g is None:
        (x,) = _mlp_down(act, w_down, layer_idx, x)
        return x, None, None
    return tuple(_mlp_down(act, w_down, layer_idx, x, g=next_norm1_g))


def _run_trunk(x, norm1_g, layers, final_norm_g):
    batch, seq, d = x.shape
    x = x.reshape(batch * seq, d)
    rope_tabs = _rope_tables(seq)
    xg, ssq = _norm_entry(x, norm1_g[0])
    for i, lw in enumerate(layers):
        next_g = norm1_g[i + 1] if i + 1 < DEPTH else None
        x, xg, ssq = _trunk_layer(x, xg, ssq, i, batch, seq, rope_tabs, next_g, *lw)
    return _rmsnorm(x, final_norm_g, F32).reshape(batch, seq, d)


def kernel(x_prompt, x_sample, norm1_g, w_in, na_rpb, diff_lq1, diff_lk1, diff_lq2, diff_lk2, diff_subln_g,
           conv_w, w_out, norm2_g, w_up, w_down, final_norm_g):
    w_in_b, w_out_b, w_up_b, w_down_b = (w.astype(BF16) for w in (w_in, w_out, w_up, w_down))
    layers = []
    for i in range(DEPTH):
        layers.append((w_in_b, _na_bias(na_rpb[i]), diff_lq1[i], diff_lk1[i], diff_lq2[i], diff_lk2[i],
                       diff_subln_g[i], conv_w[i], w_out_b, norm2_g[i], w_up_b, w_down_b))
    return (_run_trunk(x_prompt, norm1_g, layers, final_norm_g),
            _run_trunk(x_sample, norm1_g, layers, final_norm_g))
```

```python
import functools
import math

import jax
import jax.numpy as jnp
from jax import lax
from jax.experimental import pallas as pl
from jax.experimental.pallas import tpu as pltpu

F32 = jnp.float32
BF16 = jnp.bfloat16

D_MODEL = 4096
DEPTH = 4
GRID_W = 64
HEAD_DIM = 128
NA_HEADS = D_MODEL // 512
NA_WIDTH = NA_HEADS * HEAD_DIM
NA_WIN_ROWS = 8
NA_WIN_COLS = 16
DIFF_HEADS = D_MODEL // 512
DIFF_QK_DIM = HEAD_DIM
DIFF_V_DIM = 2 * DIFF_QK_DIM
DIFF_WIDTH = DIFF_HEADS * DIFF_V_DIM
CONV_WIDTH = D_MODEL - NA_WIDTH - DIFF_WIDTH
IN_WIDTH = 3 * NA_WIDTH + 3 * DIFF_WIDTH + 3 * CONV_WIDTH
D_FF = 4 * D_MODEL
ROPE_THETA = 500000.0
ROT_DIM = DIFF_QK_DIM // 4
NORM_EPS = 1e-5
NEG_INF = -1e30
LOG2_E = 1.4426950408889634

NA_COL = 0
DIFF_Q_COL = 3 * NA_WIDTH
DIFF_K_COL = DIFF_Q_COL + DIFF_WIDTH
DIFF_V_COL = DIFF_K_COL + DIFF_WIDTH
CONV_COL = DIFF_V_COL + DIFF_WIDTH

V7X_VMEM_BYTES = 64 * 1024 * 1024
V7X_VMEM_LIMIT_BYTES = V7X_VMEM_BYTES - 8 * 1024 * 1024
V7X_VMEM_LIMIT_WIDE_BYTES = V7X_VMEM_BYTES - 2 * 1024 * 1024
LANES = 128
MM_TILE = 1024


def _params(*sem, vmem_limit_bytes=V7X_VMEM_LIMIT_BYTES):
    return pltpu.CompilerParams(dimension_semantics=sem, vmem_limit_bytes=vmem_limit_bytes)


def _rmsnorm_kernel(x_ref, g_ref, o_ref):
    x = x_ref[...]
    ms = jnp.mean(x * x, axis=-1, keepdims=True)
    o_ref[...] = ((x * lax.rsqrt(ms + NORM_EPS)) * g_ref[...]).astype(o_ref.dtype)


def _rmsnorm(x, g, out_dtype, block_rows=256):
    t, d = x.shape
    return pl.pallas_call(
        _rmsnorm_kernel,
        grid=(t // block_rows,),
        in_specs=[pl.BlockSpec((block_rows, d), lambda i: (i, 0)),
                  pl.BlockSpec((1, d), lambda i: (0, 0))],
        out_specs=pl.BlockSpec((block_rows, d), lambda i: (i, 0)),
        out_shape=jax.ShapeDtypeStruct((t, d), out_dtype),
        compiler_params=_params("parallel"),
        name="rmsnorm",
    )(x, g.reshape(1, d))


def _norm_entry_kernel(x_ref, g_ref, xg_ref, ssq_ref):
    x = x_ref[...]
    xg_ref[...] = (x * g_ref[...]).astype(BF16)
    ssq_ref[...] = jnp.sum(x * x, axis=-1, keepdims=True)


def _norm_entry(x, g, block_rows=256):
    t, d = x.shape
    return pl.pallas_call(
        _norm_entry_kernel,
        grid=(t // block_rows,),
        in_specs=[pl.BlockSpec((block_rows, d), lambda i: (i, 0)),
                  pl.BlockSpec((1, d), lambda i: (0, 0))],
        out_specs=[pl.BlockSpec((block_rows, d), lambda i: (i, 0)),
                   pl.BlockSpec((block_rows, 1), lambda i: (i, 0))],
        out_shape=[jax.ShapeDtypeStruct((t, d), BF16), jax.ShapeDtypeStruct((t, 1), F32)],
        compiler_params=_params("parallel"),
        name="norm_entry",
    )(x, g.reshape(1, d))


def _row_rscale(ssq_ref):
    return lax.rsqrt(ssq_ref[...] * (1.0 / D_MODEL) + NORM_EPS)


def _rope_lanes(x, c, sa, sb):
    half = ROT_DIM // 2
    return x * c + pltpu.roll(x, DIFF_QK_DIM - half, 1) * sa + pltpu.roll(x, half, 1) * sb


def _proj_kernel(*refs, mode):
    xg_ref, w_ref, ssq_ref = refs[:3]
    o_ref = refs[-1]
    acc = jnp.dot(xg_ref[...], w_ref[...], preferred_element_type=F32) * _row_rscale(ssq_ref)
    groups = [slice(g * LANES, (g + 1) * LANES) for g in range(acc.shape[1] // LANES)]
    if mode == "f32":
        o_ref[...] = acc
    elif mode == "relu2":
        r = jnp.maximum(acc, 0.0)
        o_ref[...] = (r * r).astype(o_ref.dtype)
    elif mode == "colscale":
        o_ref[...] = (acc * refs[3][...]).astype(o_ref.dtype)
    elif mode == "rope":
        tab_ref = refs[3]
        c, sa, sb = tab_ref[0], tab_ref[1], tab_ref[2]
        for sl in groups:
            o_ref[:, sl] = _rope_lanes(acc[:, sl], c, sa, sb).astype(o_ref.dtype)
    elif mode == "transpose":
        for sl in groups:
            o_ref[0, 0, sl, :] = acc[:, sl].T.astype(o_ref.dtype)
    else:
        raise ValueError(mode)


def _project(xg, ssq, w, layer, col0, n, mode, out_dtype, extra=None, extra_spec=None, batch=None, name="proj"):
    m, k = xg.shape
    bm = bn = MM_TILE
    cb0 = col0 // bn
    in_specs = [pl.BlockSpec((bm, k), lambda i, j: (i, 0)),
                pl.BlockSpec((None, k, bn), lambda i, j: (layer, 0, cb0 + j)),
                pl.BlockSpec((bm, 1), lambda i, j: (i, 0))]
    args = [xg, w, ssq]
    if extra is not None:
        in_specs.append(extra_spec)
        args.append(extra)
    if mode == "transpose":
        nsb = m // batch // bm
        out_spec = pl.BlockSpec((1, 1, bn, bm), lambda i, j: (i // nsb, i % nsb, j, 0))
        out_shape = jax.ShapeDtypeStruct((batch, nsb, n, bm), out_dtype)
    else:
        out_spec = pl.BlockSpec((bm, bn), lambda i, j: (i, j))
        out_shape = jax.ShapeDtypeStruct((m, n), out_dtype)
    return pl.pallas_call(
        functools.partial(_proj_kernel, mode=mode),
        grid=(m // bm, n // bn),
        in_specs=in_specs,
        out_specs=out_spec,
        out_shape=out_shape,
        compiler_params=_params("parallel", "arbitrary"),
        name=name,
    )(*args)


def _emit_norm_inputs(x, g_ref, xg_ref, ssq_ref, first):
    xg_ref[...] = (x * g_ref[...]).astype(BF16)
    part = jnp.sum(x * x, axis=-1, keepdims=True)

    @pl.when(first)
    def _():
        ssq_ref[...] = part

    @pl.when(jnp.logical_not(first))
    def _():
        ssq_ref[...] += part


def _out_proj_kernel(a1_ref, a2_ref, a3_ref, w_ref, res_ref, g_ref, x_ref, xg_ref, ssq_ref, *, k_sizes):
    acc = None
    k0 = 0
    for a_ref, ks in zip((a1_ref, a2_ref, a3_ref), k_sizes):
        d = jnp.dot(a_ref[...], w_ref[k0:k0 + ks, :], preferred_element_type=F32)
        acc = d if acc is None else acc + d
        k0 += ks
    x = acc + res_ref[...]
    x_ref[...] = x
    _emit_norm_inputs(x, g_ref, xg_ref, ssq_ref, pl.program_id(1) == 0)


def _out_proj(a_list, w, layer, res, g):
    m = a_list[0].shape[0]
    k_sizes = tuple(a.shape[1] for a in a_list)
    _, k, n = w.shape
    assert sum(k_sizes) == k
    bm = bn = MM_TILE
    tile = pl.BlockSpec((bm, bn), lambda i, j: (i, j))
    return pl.pallas_call(
        functools.partial(_out_proj_kernel, k_sizes=k_sizes),
        grid=(m // bm, n // bn),
        in_specs=[pl.BlockSpec((bm, ks), lambda i, j: (i, 0)) for ks in k_sizes]
        + [pl.BlockSpec((None, k, bn), lambda i, j: (layer, 0, j)), tile, pl.BlockSpec((1, bn), lambda i, j: (0, j))],
        out_specs=[tile, tile, pl.BlockSpec((bm, 1), lambda i, j: (i, 0))],
        out_shape=[jax.ShapeDtypeStruct((m, n), F32), jax.ShapeDtypeStruct((m, n), BF16),
                   jax.ShapeDtypeStruct((m, 1), F32)],
        compiler_params=_params("parallel", "arbitrary", vmem_limit_bytes=V7X_VMEM_LIMIT_WIDE_BYTES),
        name="out_proj",
    )(*a_list, w, res, g.reshape(1, n))


def _mlp_down_kernel(*refs, emit_norm):
    if emit_norm:
        a_ref, w_ref, res_ref, g_ref, o_ref, xg_ref, ssq_ref = refs
    else:
        a_ref, w_ref, res_ref, o_ref = refs
    kk = pl.program_id(2)

    @pl.when(kk == 0)
    def _():
        o_ref[...] = res_ref[...]

    o_ref[...] += jnp.dot(a_ref[...], w_ref[...], preferred_element_type=F32)

    if emit_norm:
        @pl.when(kk == pl.num_programs(2) - 1)
        def _():
            _emit_norm_inputs(o_ref[...], g_ref, xg_ref, ssq_ref, pl.program_id(1) == 0)


def _mlp_down(a, w, layer, res, g=None, bm=MM_TILE, bn=MM_TILE, bk=4096):
    m, k = a.shape
    n = w.shape[2]
    emit_norm = g is not None
    tile = pl.BlockSpec((bm, bn), lambda i, j, kk: (i, j))
    in_specs = [pl.BlockSpec((bm, bk), lambda i, j, kk: (i, kk)),
                pl.BlockSpec((None, bk, bn), lambda i, j, kk: (layer, kk, j)),
                tile]
    args = [a, w, res]
    out_specs = [tile]
    out_shape = [jax.ShapeDtypeStruct((m, n), F32)]
    if emit_norm:
        in_specs.append(pl.BlockSpec((1, bn), lambda i, j, kk: (0, j)))
        args.append(g.reshape(1, n))
        out_specs += [tile, pl.BlockSpec((bm, 1), lambda i, j, kk: (i, 0))]
        out_shape += [jax.ShapeDtypeStruct((m, n), BF16), jax.ShapeDtypeStruct((m, 1), F32)]
    return pl.pallas_call(
        functools.partial(_mlp_down_kernel, emit_norm=emit_norm),
        grid=(m // bm, n // bn, k // bk),
        in_specs=in_specs,
        out_specs=out_specs,
        out_shape=out_shape,
        compiler_params=_params("parallel", "arbitrary", "arbitrary", vmem_limit_bytes=V7X_VMEM_LIMIT_WIDE_BYTES),
        name="mlp_down",
    )(*args)


NA_BIAS_VARIANTS = NA_WIN_ROWS
NA_BAND_KEYS = NA_WIN_ROWS * GRID_W
RPB_ROWS = 2 * NA_WIN_ROWS - 1
RPB_COLS = 2 * NA_WIN_COLS - 1
NA_ROW_GROUP = 16


def _na_bias_kernel(rpb_ref, o_ref):
    h = pl.program_id(0)
    shape = (GRID_W, 2 * GRID_W)
    lane = lax.broadcasted_iota(jnp.int32, shape, 1)
    cq = lax.broadcasted_iota(jnp.int32, shape, 0)
    ck = lane & (GRID_W - 1)
    upper = lane >= GRID_W
    dcl = jnp.clip(ck - cq, -(NA_WIN_COLS - 1), NA_WIN_COLS - 1) + (NA_WIN_COLS - 1)
    c0 = jnp.clip(cq - NA_WIN_COLS // 2, 0, GRID_W - NA_WIN_COLS)
    col_in = jnp.logical_and(ck >= c0, ck < c0 + NA_WIN_COLS)
    pairs = []
    for d in range(RPB_ROWS - 1):
        acc = jnp.zeros(shape, F32)
        for dc in range(RPB_COLS):
            lo = rpb_ref[(h * RPB_ROWS + d) * RPB_COLS + dc]
            hi = rpb_ref[(h * RPB_ROWS + d + 1) * RPB_COLS + dc]
            acc = jnp.where(dcl == dc, jnp.where(upper, hi, lo), acc)
        pairs.append(jnp.where(col_in, acc, NEG_INF))
    for off in range(NA_BIAS_VARIANTS):
        for jj in range(NA_WIN_ROWS // 2):
            o_ref[0, off, :, jj * 2 * GRID_W:(jj + 1) * 2 * GRID_W] = pairs[off + 2 * jj]


def _na_bias(rpb):
    return pl.pallas_call(
        _na_bias_kernel,
        grid=(NA_HEADS,),
        in_specs=[pl.BlockSpec(memory_space=pltpu.SMEM)],
        out_specs=pl.BlockSpec((1, NA_BIAS_VARIANTS, GRID_W, NA_BAND_KEYS), lambda h: (h, 0, 0, 0)),
        out_shape=jax.ShapeDtypeStruct((NA_HEADS, NA_BIAS_VARIANTS, GRID_W, NA_BAND_KEYS), F32),
        compiler_params=_params("arbitrary"),
        name="na_bias",
    )(rpb.reshape(-1))


def _na_kernel(q_ref, k_ref, v_ref, bias_ref, o_ref, *, seq):
    rows = seq // GRID_W

    def group_body(gi, carry):
        qsls, ksls, scores = [], [], []
        for u in range(NA_ROW_GROUP):
            r = gi * NA_ROW_GROUP + u
            r0 = jnp.clip(r - NA_WIN_ROWS // 2, 0, rows - NA_WIN_ROWS)
            off = r0 - r + (NA_WIN_ROWS - 1)
            qsl = pl.ds(pl.multiple_of(r * GRID_W, GRID_W), GRID_W)
            ksl = pl.ds(pl.multiple_of(r0 * GRID_W, GRID_W), NA_BAND_KEYS)
            s = lax.dot_general(q_ref[qsl, :], k_ref[ksl, :], (((1,), (1,)), ((), ())),
                                preferred_element_type=F32)
            scores.append(s + bias_ref[0, off])
            qsls.append(qsl)
            ksls.append(ksl)
        probs, inv_ls = [], []
        for s in scores:
            p = jnp.exp(s - jnp.max(s, axis=-1, keepdims=True))
            inv_ls.append(1.0 / jnp.sum(p, axis=-1, keepdims=True))
            probs.append(p.astype(BF16))
        for qsl, ksl, p, inv_l in zip(qsls, ksls, probs, inv_ls):
            o = jnp.dot(p, v_ref[ksl, :], preferred_element_type=F32)
            o_ref[qsl, :] = (o * inv_l).astype(o_ref.dtype)
        return carry

    lax.fori_loop(0, rows // NA_ROW_GROUP, group_body, 0)


def _na_attention(qkv, bias, batch, seq):
    t = qkv.shape[0]
    return pl.pallas_call(
        functools.partial(_na_kernel, seq=seq),
        grid=(batch, NA_HEADS),
        in_specs=[pl.BlockSpec((seq, HEAD_DIM), lambda b, h: (b, h)),
                  pl.BlockSpec((seq, HEAD_DIM), lambda b, h: (b, NA_HEADS + h)),
                  pl.BlockSpec((seq, HEAD_DIM), lambda b, h: (b, 2 * NA_HEADS + h)),
                  pl.BlockSpec((1, NA_BIAS_VARIANTS, GRID_W, NA_BAND_KEYS), lambda b, h: (h, 0, 0, 0))],
        out_specs=pl.BlockSpec((seq, HEAD_DIM), lambda b, h: (b, h)),
        out_shape=jax.ShapeDtypeStruct((t, NA_WIDTH), BF16),
        compiler_params=_params("parallel", "parallel"),
        name="na_attention",
    )(qkv, qkv, qkv, bias)


DIFF_K_ROWS = MM_TILE
DIFF_Q_ROWS = 256
DIFF_Q_UNROLL = 4


def _rope_tables(seq):
    half = ROT_DIM // 2
    inv_freq = ROPE_THETA ** (-jnp.arange(half, dtype=F32) / half)
    ang = jnp.arange(seq, dtype=F32)[:, None] * inv_freq[None, :]
    cos, sin = jnp.cos(ang), jnp.sin(ang)
    rest = DIFF_QK_DIM - ROT_DIM
    c = jnp.concatenate([cos, cos, jnp.ones((seq, rest), F32)], axis=1)
    sa = jnp.concatenate([-sin, jnp.zeros((seq, DIFF_QK_DIM - half), F32)], axis=1)
    sb = jnp.concatenate([jnp.zeros((seq, half), F32), sin, jnp.zeros((seq, rest), F32)], axis=1)
    k_tabs = jnp.stack([c, sa, sb])
    return jnp.stack([k_tabs * (DIFF_QK_DIM ** -0.5 * LOG2_E), k_tabs])


def _diff_kernel(lam_ref, q_ref, k_ref, vt_ref, lq1_ref, lk1_ref, lq2_ref, lk2_ref, g_ref, o_ref, acc_ref, s_ref,
                 *, seq):
    bk = DIFF_K_ROWS
    bq = DIFF_Q_ROWS
    n_chunks = seq // bk
    n_q = seq // bq
    assert n_chunks % 2 == 0
    csl = [slice(c * DIFF_QK_DIM, (c + 1) * DIFF_QK_DIM) for c in range(2)]
    lam_init = lam_ref[0]
    lam = (jnp.exp(jnp.sum(lq1_ref[...] * lk1_ref[...], axis=-1, keepdims=True))
           - jnp.exp(jnp.sum(lq2_ref[...] * lk2_ref[...], axis=-1, keepdims=True)) + lam_init)

    def scores(qi, j, slot):
        qsl = pl.ds(pl.multiple_of(qi * bq, bq), bq)
        for c in range(2):
            s_ref[slot, c] = lax.dot_general(k_ref[j * bk:(j + 1) * bk, csl[c]], q_ref[qsl, csl[c]],
                                             (((1,), (1,)), ((), ())), preferred_element_type=F32)

    def consume(j, slot, carry):
        vt = vt_ref[0, j]
        new_ms, new_ls = [], []
        for c in range(2):
            st = s_ref[slot, c]
            m_blk = jnp.max(st, axis=0, keepdims=True)
            if j == 0:
                m_new = m_blk
                p = jnp.exp2(st - m_new)
                new_ls.append(jnp.sum(p, axis=0, keepdims=True))
                acc_ref[c] = jnp.dot(vt, p.astype(BF16), preferred_element_type=F32)
            else:
                ms, ls = carry
                m_new = jnp.maximum(ms[c], m_blk)
                alpha = jnp.exp2(ms[c] - m_new)
                p = jnp.exp2(st - m_new)
                new_ls.append(alpha * ls[c] + jnp.sum(p, axis=0, keepdims=True))
                acc_ref[c] = alpha * acc_ref[c] + jnp.dot(vt, p.astype(BF16), preferred_element_type=F32)
            new_ms.append(m_new)
        return tuple(new_ms), tuple(new_ls)

    def q_block(qi, _):
        carry = None
        for j in range(n_chunks):
            slot = j % 2
            if j + 1 < n_chunks:
                scores(qi, j + 1, 1 - slot)
            else:
                scores(jnp.minimum(qi + 1, n_q - 1), 0, 1 - slot)
            carry = consume(j, slot, carry)
        _, ls = carry
        o = acc_ref[0] * (1.0 / ls[0]) - lam * (acc_ref[1] * (1.0 / ls[1]))
        ms = jnp.mean(o * o, axis=0, keepdims=True)
        y = (o * lax.rsqrt(ms + NORM_EPS)).T * g_ref[...]
        o_ref[pl.ds(pl.multiple_of(qi * bq, bq), bq), :] = (y * lam_ref[1]).astype(o_ref.dtype)
        return 0

    scores(0, 0, 0)
    lax.fori_loop(0, n_q, q_block, 0, unroll=DIFF_Q_UNROLL)


def _diff_attention(qk, vt, lam_consts, lq1, lk1, lq2, lk2, subln_g, batch, seq):
    t = qk.shape[0]
    bq = DIFF_Q_ROWS
    nkb = seq // DIFF_K_ROWS
    vec = lambda n: pl.BlockSpec((1, n), lambda b, h: (0, 0))
    return pl.pallas_call(
        functools.partial(_diff_kernel, seq=seq),
        grid=(batch, DIFF_HEADS),
        in_specs=[pl.BlockSpec(memory_space=pltpu.SMEM),
                  pl.BlockSpec((seq, DIFF_V_DIM), lambda b, h: (b, h)),
                  pl.BlockSpec((seq, DIFF_V_DIM), lambda b, h: (b, DIFF_HEADS + h)),
                  pl.BlockSpec((1, nkb, DIFF_V_DIM, DIFF_K_ROWS), lambda b, h: (b, 0, h, 0)),
                  vec(DIFF_QK_DIM), vec(DIFF_QK_DIM), vec(DIFF_QK_DIM), vec(DIFF_QK_DIM), vec(DIFF_V_DIM)],
        out_specs=pl.BlockSpec((seq, DIFF_V_DIM), lambda b, h: (b, h)),
        out_shape=jax.ShapeDtypeStruct((t, DIFF_WIDTH), BF16),
        scratch_shapes=[pltpu.VMEM((2, DIFF_V_DIM, bq), F32),
                        pltpu.VMEM((2, 2, DIFF_K_ROWS, bq), F32)],
        compiler_params=_params("parallel", "parallel"),
        name="diff_attention",
    )(lam_consts, qk, qk, vt, lq1.reshape(1, -1), lk1.reshape(1, -1), lq2.reshape(1, -1), lk2.reshape(1, -1),
      subln_g.reshape(1, -1))


CONV_CHUNK = 512


def _conv_kernel(u_ref, b_ref, c_ref, w_ref, o_ref, *, seq):
    n_chunks = seq // CONV_CHUNK
    w0, w1, w2 = w_ref[0:1, :], w_ref[1:2, :], w_ref[2:3, :]
    row = lax.broadcasted_iota(jnp.int32, (CONV_CHUNK, LANES), 0)

    def body(ci, carry):
        t0 = pl.multiple_of(ci * CONV_CHUNK, CONV_CHUNK)
        sl = pl.ds(t0, CONV_CHUNK)
        z = c_ref[sl, :] * u_ref[sl, :]
        psl = pl.ds(pl.multiple_of(jnp.maximum(t0 - 8, 0), 8), 8)
        nsl = pl.ds(pl.multiple_of(jnp.minimum(t0 + CONV_CHUNK, seq - 8), 8), 8)
        zp = (c_ref[psl, :] * u_ref[psl, :])[7:8, :]
        zn = (c_ref[nsl, :] * u_ref[nsl, :])[0:1, :]
        zp = jnp.where(ci > 0, zp, 0.0)
        zn = jnp.where(ci < n_chunks - 1, zn, 0.0)
        z_prev = jnp.where(row == 0, zp, pltpu.roll(z, 1, 0))
        z_next = jnp.where(row == CONV_CHUNK - 1, zn, pltpu.roll(z, CONV_CHUNK - 1, 0))
        y = w0 * z_prev + w1 * z + w2 * z_next
        o_ref[sl, :] = (b_ref[sl, :] * y).astype(o_ref.dtype)
        return carry

    lax.fori_loop(0, n_chunks, body, 0)


def _short_conv(ubc, conv_w, batch, seq):
    t = ubc.shape[0]
    nb = CONV_WIDTH // LANES
    return pl.pallas_call(
        functools.partial(_conv_kernel, seq=seq),
        grid=(batch, nb),
        in_specs=[pl.BlockSpec((seq, LANES), lambda b, j: (b, j)),
                  pl.BlockSpec((seq, LANES), lambda b, j: (b, nb + j)),
                  pl.BlockSpec((seq, LANES), lambda b, j: (b, 2 * nb + j)),
                  pl.BlockSpec((3, LANES), lambda b, j: (0, j))],
        out_specs=pl.BlockSpec((seq, LANES), lambda b, j: (b, j)),
        out_shape=jax.ShapeDtypeStruct((t, CONV_WIDTH), BF16),
        compiler_params=_params("parallel", "parallel"),
        name="short_conv",
    )(ubc, ubc, ubc, conv_w)


def _na_col_scale():
    return jnp.concatenate([jnp.full((1, NA_WIDTH), HEAD_DIM ** -0.5, F32), jnp.ones((1, 2 * NA_WIDTH), F32)], axis=1)


def _trunk_layer(x, xg, ssq, layer_idx, batch, seq, rope_tabs, next_norm1_g, w_in, na_bias, lq1, lk1, lq2, lk2,
                 subln_g, conv_w, w_out, norm2_g, w_up, w_down):
    nsb = seq // MM_TILE
    na_qkv = _project(xg, ssq, w_in, layer_idx, NA_COL, 3 * NA_WIDTH, "colscale", BF16, extra=_na_col_scale(),
                      extra_spec=pl.BlockSpec((1, MM_TILE), lambda i, j: (0, j)), name="in_proj_na")
    heads_per_tile = DIFF_WIDTH // MM_TILE
    diff_qk = _project(xg, ssq, w_in, layer_idx, DIFF_Q_COL, 2 * DIFF_WIDTH, "rope", BF16, extra=rope_tabs,
                       extra_spec=pl.BlockSpec((None, 3, MM_TILE, DIFF_QK_DIM),
                                               lambda i, j: (j // heads_per_tile, 0, i % nsb, 0)),
                       name="in_proj_qk")
    diff_vt = _project(xg, ssq, w_in, layer_idx, DIFF_V_COL, DIFF_WIDTH, "transpose", BF16, batch=batch,
                       name="in_proj_vt")
    conv_ubc = _project(xg, ssq, w_in, layer_idx, CONV_COL, 3 * CONV_WIDTH, "f32", F32, name="in_proj_conv")

    na_out = _na_attention(na_qkv, na_bias, batch, seq)
    lam_init = 0.8 - 0.6 * math.exp(-0.3 * layer_idx)
    lam_consts = jnp.array([lam_init, 1.0 - lam_init], F32)
    diff_out = _diff_attention(diff_qk, diff_vt, lam_consts, lq1, lk1, lq2, lk2, subln_g, batch, seq)
    conv_out = _short_conv(conv_ubc, conv_w, batch, seq)

    x, xg2, ssq2 = _out_proj([na_out, diff_out, conv_out], w_out, layer_idx, x, norm2_g)
    act = _project(xg2, ssq2, w_up, layer_idx, 0, D_FF, "relu2", BF16, name="mlp_up")
    if next_norm1_g is None:
        (x,) = _mlp_down(act, w_down, layer_idx, x)
        return x, None, None
    return tuple(_mlp_down(act, w_down, layer_idx, x, g=next_norm1_g))


def _run_trunk(x, norm1_g, layers, final_norm_g):
    batch, seq, d = x.shape
    x = x.reshape(batch * seq, d)
    rope_tabs = _rope_tables(seq)
    xg, ssq = _norm_entry(x, norm1_g[0])
    for i, lw in enumerate(layers):
        next_g = norm1_g[i + 1] if i + 1 < DEPTH else None
        x, xg, ssq = _trunk_layer(x, xg, ssq, i, batch, seq, rope_tabs, next_g, *lw)
    return _rmsnorm(x, final_norm_g, F32).reshape(batch, seq, d)


def kernel(x_prompt, x_sample, norm1_g, w_in, na_rpb, diff_lq1, diff_lk1, diff_lq2, diff_lk2, diff_subln_g,
           conv_w, w_out, norm2_g, w_up, w_down, final_norm_g):
    w_in_b, w_out_b, w_up_b, w_down_b = (w.astype(BF16) for w in (w_in, w_out, w_up, w_down))
    layers = []
    for i in range(DEPTH):
        layers.append((w_in_b, _na_bias(na_rpb[i]), diff_lq1[i], diff_lk1[i], diff_lq2[i], diff_lk2[i],
                       diff_subln_g[i], conv_w[i], w_out_b, norm2_g[i], w_up_b, w_down_b))
    return (_run_trunk(x_prompt, norm1_g, layers, final_norm_g),
            _run_trunk(x_sample, norm1_g, layers, final_norm_g))
```

```python
import functools
import math

import jax
import jax.numpy as jnp
from jax import lax
from jax.experimental import pallas as pl
from jax.experimental.pallas import tpu as pltpu

F32 = jnp.float32
BF16 = jnp.bfloat16

D_MODEL = 4096
DEPTH = 4
GRID_W = 64
HEAD_DIM = 128
NA_HEADS = D_MODEL // 512
NA_WIDTH = NA_HEADS * HEAD_DIM
NA_WIN_ROWS = 8
NA_WIN_COLS = 16
DIFF_HEADS = D_MODEL // 512
DIFF_QK_DIM = HEAD_DIM
DIFF_V_DIM = 2 * DIFF_QK_DIM
DIFF_WIDTH = DIFF_HEADS * DIFF_V_DIM
CONV_WIDTH = D_MODEL - NA_WIDTH - DIFF_WIDTH
IN_WIDTH = 3 * NA_WIDTH + 3 * DIFF_WIDTH + 3 * CONV_WIDTH
D_FF = 4 * D_MODEL
ROPE_THETA = 500000.0
ROT_DIM = DIFF_QK_DIM // 4
NORM_EPS = 1e-5
NEG_INF = -1e30
LOG2_E = 1.4426950408889634

NA_COL = 0
DIFF_Q_COL = 3 * NA_WIDTH
DIFF_K_COL = DIFF_Q_COL + DIFF_WIDTH
DIFF_V_COL = DIFF_K_COL + DIFF_WIDTH
CONV_COL = DIFF_V_COL + DIFF_WIDTH

V7X_VMEM_BYTES = 64 * 1024 * 1024
V7X_VMEM_LIMIT_BYTES = V7X_VMEM_BYTES - 8 * 1024 * 1024
V7X_VMEM_LIMIT_WIDE_BYTES = V7X_VMEM_BYTES - 2 * 1024 * 1024
LANES = 128
MM_TILE = 1024


def _params(*sem, vmem_limit_bytes=V7X_VMEM_LIMIT_BYTES):
    return pltpu.CompilerParams(dimension_semantics=sem, vmem_limit_bytes=vmem_limit_bytes)


def _rmsnorm_kernel(x_ref, g_ref, o_ref):
    x = x_ref[...]
    ms = jnp.mean(x * x, axis=-1, keepdims=True)
    o_ref[...] = ((x * lax.rsqrt(ms + NORM_EPS)) * g_ref[...]).astype(o_ref.dtype)


def _rmsnorm(x, g, out_dtype, block_rows=256):
    t, d = x.shape
    return pl.pallas_call(
        _rmsnorm_kernel,
        grid=(t // block_rows,),
        in_specs=[pl.BlockSpec((block_rows, d), lambda i: (i, 0)),
                  pl.BlockSpec((1, d), lambda i: (0, 0))],
        out_specs=pl.BlockSpec((block_rows, d), lambda i: (i, 0)),
        out_shape=jax.ShapeDtypeStruct((t, d), out_dtype),
        compiler_params=_params("parallel"),
        name="rmsnorm",
    )(x, g.reshape(1, d))


def _norm_entry_kernel(x_ref, g_ref, xg_ref, ssq_ref):
    x = x_ref[...]
    xg_ref[...] = (x * g_ref[...]).astype(BF16)
    ssq_ref[...] = jnp.sum(x * x, axis=-1, keepdims=True)


def _norm_entry(x, g, block_rows=256):
    t, d = x.shape
    return pl.pallas_call(
        _norm_entry_kernel,
        grid=(t // block_rows,),
        in_specs=[pl.BlockSpec((block_rows, d), lambda i: (i, 0)),
                  pl.BlockSpec((1, d), lambda i: (0, 0))],
        out_specs=[pl.BlockSpec((block_rows, d), lambda i: (i, 0)),
                   pl.BlockSpec((block_rows, 1), lambda i: (i, 0))],
        out_shape=[jax.ShapeDtypeStruct((t, d), BF16), jax.ShapeDtypeStruct((t, 1), F32)],
        compiler_params=_params("parallel"),
        name="norm_entry",
    )(x, g.reshape(1, d))


def _row_rscale(ssq_ref):
    return lax.rsqrt(ssq_ref[...] * (1.0 / D_MODEL) + NORM_EPS)


def _rope_lanes(x, c, sa, sb):
    half = ROT_DIM // 2
    return x * c + pltpu.roll(x, DIFF_QK_DIM - half, 1) * sa + pltpu.roll(x, half, 1) * sb


def _proj_kernel(*refs, mode):
    xg_ref, w_ref, ssq_ref = refs[:3]
    o_ref = refs[-1]
    acc = jnp.dot(xg_ref[...], w_ref[...], preferred_element_type=F32) * _row_rscale(ssq_ref)
    groups = [slice(g * LANES, (g + 1) * LANES) for g in range(acc.shape[1] // LANES)]
    if mode == "f32":
        o_ref[...] = acc
    elif mode == "colscale":
        o_ref[...] = (acc * refs[3][...]).astype(o_ref.dtype)
    elif mode == "rope":
        tab_ref = refs[3]
        c, sa, sb = tab_ref[0], tab_ref[1], tab_ref[2]
        for sl in groups:
            o_ref[:, sl] = _rope_lanes(acc[:, sl], c, sa, sb).astype(o_ref.dtype)
    elif mode == "transpose":
        for sl in groups:
            o_ref[0, 0, sl, :] = acc[:, sl].T.astype(o_ref.dtype)
    else:
        raise ValueError(mode)


def _project(xg, ssq, w, col0, n, mode, out_dtype, extra=None, extra_spec=None, batch=None, name="proj"):
    m, k = xg.shape
    bm = bn = MM_TILE
    cb0 = col0 // bn
    in_specs = [pl.BlockSpec((bm, k), lambda i, j: (i, 0)),
                pl.BlockSpec((k, bn), lambda i, j: (0, cb0 + j)),
                pl.BlockSpec((bm, 1), lambda i, j: (i, 0))]
    args = [xg, w, ssq]
    if extra is not None:
        in_specs.append(extra_spec)
        args.append(extra)
    if mode == "transpose":
        nsb = m // batch // bm
        out_spec = pl.BlockSpec((1, 1, bn, bm), lambda i, j: (i // nsb, i % nsb, j, 0))
        out_shape = jax.ShapeDtypeStruct((batch, nsb, n, bm), out_dtype)
    else:
        out_spec = pl.BlockSpec((bm, bn), lambda i, j: (i, j))
        out_shape = jax.ShapeDtypeStruct((m, n), out_dtype)
    return pl.pallas_call(
        functools.partial(_proj_kernel, mode=mode),
        grid=(m // bm, n // bn),
        in_specs=in_specs,
        out_specs=out_spec,
        out_shape=out_shape,
        compiler_params=_params("parallel", "arbitrary"),
        name=name,
    )(*args)


CAST_TILE_ROWS = 512


def _mlp_up_kernel(*refs, n_cast):
    xg_ref, w_ref, ssq_ref = refs[:3]
    src_refs = refs[3:3 + n_cast]
    act_ref = refs[3 + n_cast]
    dst_refs = refs[4 + n_cast:]
    for src_ref, dst_ref in zip(src_refs, dst_refs):
        dst_ref[...] = src_ref[...].astype(dst_ref.dtype)
    acc = jnp.dot(xg_ref[...], w_ref[...], preferred_element_type=F32) * _row_rscale(ssq_ref)
    r = jnp.maximum(acc, 0.0)
    act_ref[...] = (r * r).astype(act_ref.dtype)


def _mlp_up(xg, ssq, w, cast_layer=None, cast_srcs=()):
    m, k = xg.shape
    n = w.shape[1]
    bm = bn = MM_TILE
    gm, gn = m // bm, n // bn
    steps = gm * gn
    tile = pl.BlockSpec((bm, bn), lambda i, j: (i, j))
    in_specs = [pl.BlockSpec((bm, k), lambda i, j: (i, 0)),
                pl.BlockSpec((k, bn), lambda i, j: (0, j)),
                pl.BlockSpec((bm, 1), lambda i, j: (i, 0))]
    out_specs = [tile]
    out_shape = [jax.ShapeDtypeStruct((m, n), BF16)]
    for src in cast_srcs:
        _, rows, cols = src.shape
        rb = CAST_TILE_ROWS
        cb = rows * cols // (steps * rb)
        ct = cols // cb
        assert cb % LANES == 0 and cols % cb == 0 and (rows // rb) * ct == steps
        in_specs.append(pl.BlockSpec((None, rb, cb),
                                     lambda i, j, ct=ct: (cast_layer, (i * gn + j) // ct, (i * gn + j) % ct)))
        out_specs.append(pl.BlockSpec((rb, cb), lambda i, j, ct=ct: ((i * gn + j) // ct, (i * gn + j) % ct)))
        out_shape.append(jax.ShapeDtypeStruct((rows, cols), BF16))
    limit = V7X_VMEM_LIMIT_WIDE_BYTES if cast_srcs else V7X_VMEM_LIMIT_BYTES
    outs = pl.pallas_call(
        functools.partial(_mlp_up_kernel, n_cast=len(cast_srcs)),
        grid=(gm, gn),
        in_specs=in_specs,
        out_specs=out_specs,
        out_shape=out_shape,
        compiler_params=_params("parallel", "arbitrary", vmem_limit_bytes=limit),
        name="mlp_up",
    )(xg, w, ssq, *cast_srcs)
    return outs[0], tuple(outs[1:])


def _emit_norm_inputs(x, g_ref, xg_ref, ssq_ref, first):
    xg_ref[...] = (x * g_ref[...]).astype(BF16)
    part = jnp.sum(x * x, axis=-1, keepdims=True)

    @pl.when(first)
    def _():
        ssq_ref[...] = part

    @pl.when(jnp.logical_not(first))
    def _():
        ssq_ref[...] += part


def _out_proj_kernel(a1_ref, a2_ref, a3_ref, w_ref, res_ref, g_ref, x_ref, xg_ref, ssq_ref, *, k_sizes):
    acc = None
    k0 = 0
    for a_ref, ks in zip((a1_ref, a2_ref, a3_ref), k_sizes):
        d = jnp.dot(a_ref[...], w_ref[k0:k0 + ks, :], preferred_element_type=F32)
        acc = d if acc is None else acc + d
        k0 += ks
    x = acc + res_ref[...]
    x_ref[...] = x
    _emit_norm_inputs(x, g_ref, xg_ref, ssq_ref, pl.program_id(1) == 0)


def _out_proj(a_list, w, res, g):
    m = a_list[0].shape[0]
    k_sizes = tuple(a.shape[1] for a in a_list)
    k, n = w.shape
    assert sum(k_sizes) == k
    bm = bn = MM_TILE
    tile = pl.BlockSpec((bm, bn), lambda i, j: (i, j))
    return pl.pallas_call(
        functools.partial(_out_proj_kernel, k_sizes=k_sizes),
        grid=(m // bm, n // bn),
        in_specs=[pl.BlockSpec((bm, ks), lambda i, j: (i, 0)) for ks in k_sizes]
        + [pl.BlockSpec((k, bn), lambda i, j: (0, j)), tile, pl.BlockSpec((1, bn), lambda i, j: (0, j))],
        out_specs=[tile, tile, pl.BlockSpec((bm, 1), lambda i, j: (i, 0))],
        out_shape=[jax.ShapeDtypeStruct((m, n), F32), jax.ShapeDtypeStruct((m, n), BF16),
                   jax.ShapeDtypeStruct((m, 1), F32)],
        compiler_params=_params("parallel", "arbitrary", vmem_limit_bytes=V7X_VMEM_LIMIT_WIDE_BYTES),
        name="out_proj",
    )(*a_list, w, res, g.reshape(1, n))


def _mlp_down_kernel(*refs, emit_norm):
    if emit_norm:
        a_ref, w_ref, res_ref, g_ref, o_ref, xg_ref, ssq_ref = refs
    else:
        a_ref, w_ref, res_ref, o_ref = refs
    kk = pl.program_id(2)

    @pl.when(kk == 0)
    def _():
        o_ref[...] = res_ref[...]

    o_ref[...] += jnp.dot(a_ref[...], w_ref[...], preferred_element_type=F32)

    if emit_norm:
        @pl.when(kk == pl.num_programs(2) - 1)
        def _():
            _emit_norm_inputs(o_ref[...], g_ref, xg_ref, ssq_ref, pl.program_id(1) == 0)


def _mlp_down(a, w, res, g=None, bm=MM_TILE, bn=MM_TILE, bk=4096):
    m, k = a.shape
    n = w.shape[1]
    emit_norm = g is not None
    tile = pl.BlockSpec((bm, bn), lambda i, j, kk: (i, j))
    in_specs = [pl.BlockSpec((bm, bk), lambda i, j, kk: (i, kk)),
                pl.BlockSpec((bk, bn), lambda i, j, kk: (kk, j)),
                tile]
    args = [a, w, res]
    out_specs = [tile]
    out_shape = [jax.ShapeDtypeStruct((m, n), F32)]
    if emit_norm:
        in_specs.append(pl.BlockSpec((1, bn), lambda i, j, kk: (0, j)))
        args.append(g.reshape(1, n))
        out_specs += [tile, pl.BlockSpec((bm, 1), lambda i, j, kk: (i, 0))]
        out_shape += [jax.ShapeDtypeStruct((m, n), BF16), jax.ShapeDtypeStruct((m, 1), F32)]
    return pl.pallas_call(
        functools.partial(_mlp_down_kernel, emit_norm=emit_norm),
        grid=(m // bm, n // bn, k // bk),
        in_specs=in_specs,
        out_specs=out_specs,
        out_shape=out_shape,
        compiler_params=_params("parallel", "arbitrary", "arbitrary", vmem_limit_bytes=V7X_VMEM_LIMIT_WIDE_BYTES),
        name="mlp_down",
    )(*args)


NA_BIAS_VARIANTS = NA_WIN_ROWS
NA_BAND_KEYS = NA_WIN_ROWS * GRID_W
RPB_ROWS = 2 * NA_WIN_ROWS - 1
RPB_COLS = 2 * NA_WIN_COLS - 1
NA_ROW_GROUP = 16


def _na_bias_kernel(rpb_ref, o_ref):
    h = pl.program_id(0)
    shape = (GRID_W, 2 * GRID_W)
    lane = lax.broadcasted_iota(jnp.int32, shape, 1)
    cq = lax.broadcasted_iota(jnp.int32, shape, 0)
    ck = lane & (GRID_W - 1)
    upper = lane >= GRID_W
    dcl = jnp.clip(ck - cq, -(NA_WIN_COLS - 1), NA_WIN_COLS - 1) + (NA_WIN_COLS - 1)
    c0 = jnp.clip(cq - NA_WIN_COLS // 2, 0, GRID_W - NA_WIN_COLS)
    col_in = jnp.logical_and(ck >= c0, ck < c0 + NA_WIN_COLS)
    pairs = []
    for d in range(RPB_ROWS - 1):
        acc = jnp.zeros(shape, F32)
        for dc in range(RPB_COLS):
            lo = rpb_ref[(h * RPB_ROWS + d) * RPB_COLS + dc]
            hi = rpb_ref[(h * RPB_ROWS + d + 1) * RPB_COLS + dc]
            acc = jnp.where(dcl == dc, jnp.where(upper, hi, lo), acc)
        pairs.append(jnp.where(col_in, acc, NEG_INF))
    for off in range(NA_BIAS_VARIANTS):
        for jj in range(NA_WIN_ROWS // 2):
            o_ref[0, off, :, jj * 2 * GRID_W:(jj + 1) * 2 * GRID_W] = pairs[off + 2 * jj]


def _na_bias(rpb):
    return pl.pallas_call(
        _na_bias_kernel,
        grid=(NA_HEADS,),
        in_specs=[pl.BlockSpec(memory_space=pltpu.SMEM)],
        out_specs=pl.BlockSpec((1, NA_BIAS_VARIANTS, GRID_W, NA_BAND_KEYS), lambda h: (h, 0, 0, 0)),
        out_shape=jax.ShapeDtypeStruct((NA_HEADS, NA_BIAS_VARIANTS, GRID_W, NA_BAND_KEYS), F32),
        compiler_params=_params("arbitrary"),
        name="na_bias",
    )(rpb.reshape(-1))


def _na_kernel(q_ref, k_ref, v_ref, bias_ref, o_ref, *, seq):
    rows = seq // GRID_W

    def group_body(gi, carry):
        qsls, ksls, scores = [], [], []
        for u in range(NA_ROW_GROUP):
            r = gi * NA_ROW_GROUP + u
            r0 = jnp.clip(r - NA_WIN_ROWS // 2, 0, rows - NA_WIN_ROWS)
            off = r0 - r + (NA_WIN_ROWS - 1)
            qsl = pl.ds(pl.multiple_of(r * GRID_W, GRID_W), GRID_W)
            ksl = pl.ds(pl.multiple_of(r0 * GRID_W, GRID_W), NA_BAND_KEYS)
            s = lax.dot_general(q_ref[qsl, :], k_ref[ksl, :], (((1,), (1,)), ((), ())),
                                preferred_element_type=F32)
            scores.append(s + bias_ref[0, off])
            qsls.append(qsl)
            ksls.append(ksl)
        probs, inv_ls = [], []
        for s in scores:
            p = jnp.exp(s - jnp.max(s, axis=-1, keepdims=True))
            inv_ls.append(1.0 / jnp.sum(p, axis=-1, keepdims=True))
            probs.append(p.astype(BF16))
        for qsl, ksl, p, inv_l in zip(qsls, ksls, probs, inv_ls):
            o = jnp.dot(p, v_ref[ksl, :], preferred_element_type=F32)
            o_ref[qsl, :] = (o * inv_l).astype(o_ref.dtype)
        return carry

    lax.fori_loop(0, rows // NA_ROW_GROUP, group_body, 0)


def _na_attention(qkv, bias, batch, seq):
    t = qkv.shape[0]
    return pl.pallas_call(
        functools.partial(_na_kernel, seq=seq),
        grid=(batch, NA_HEADS),
        in_specs=[pl.BlockSpec((seq, HEAD_DIM), lambda b, h: (b, h)),
                  pl.BlockSpec((seq, HEAD_DIM), lambda b, h: (b, NA_HEADS + h)),
                  pl.BlockSpec((seq, HEAD_DIM), lambda b, h: (b, 2 * NA_HEADS + h)),
                  pl.BlockSpec((1, NA_BIAS_VARIANTS, GRID_W, NA_BAND_KEYS), lambda b, h: (h, 0, 0, 0))],
        out_specs=pl.BlockSpec((seq, HEAD_DIM), lambda b, h: (b, h)),
        out_shape=jax.ShapeDtypeStruct((t, NA_WIDTH), BF16),
        compiler_params=_params("parallel", "parallel"),
        name="na_attention",
    )(qkv, qkv, qkv, bias)


DIFF_K_ROWS = MM_TILE
DIFF_Q_ROWS = 256
DIFF_Q_UNROLL = 4


def _rope_tables(seq):
    half = ROT_DIM // 2
    inv_freq = ROPE_THETA ** (-jnp.arange(half, dtype=F32) / half)
    ang = jnp.arange(seq, dtype=F32)[:, None] * inv_freq[None, :]
    cos, sin = jnp.cos(ang), jnp.sin(ang)
    rest = DIFF_QK_DIM - ROT_DIM
    c = jnp.concatenate([cos, cos, jnp.ones((seq, rest), F32)], axis=1)
    sa = jnp.concatenate([-sin, jnp.zeros((seq, DIFF_QK_DIM - half), F32)], axis=1)
    sb = jnp.concatenate([jnp.zeros((seq, half), F32), sin, jnp.zeros((seq, rest), F32)], axis=1)
    k_tabs = jnp.stack([c, sa, sb])
    return jnp.stack([k_tabs * (DIFF_QK_DIM ** -0.5 * LOG2_E), k_tabs])


def _diff_kernel(lam_ref, q_ref, k_ref, vt_ref, lq1_ref, lk1_ref, lq2_ref, lk2_ref, g_ref, o_ref, acc_ref, s_ref,
                 *, seq):
    bk = DIFF_K_ROWS
    bq = DIFF_Q_ROWS
    n_chunks = seq // bk
    n_q = seq // bq
    assert n_chunks % 2 == 0
    csl = [slice(c * DIFF_QK_DIM, (c + 1) * DIFF_QK_DIM) for c in range(2)]
    lam_init = lam_ref[0]
    lam = (jnp.exp(jnp.sum(lq1_ref[...] * lk1_ref[...], axis=-1, keepdims=True))
           - jnp.exp(jnp.sum(lq2_ref[...] * lk2_ref[...], axis=-1, keepdims=True)) + lam_init)

    def scores(qi, j, slot):
        qsl = pl.ds(pl.multiple_of(qi * bq, bq), bq)
        for c in range(2):
            s_ref[slot, c] = lax.dot_general(k_ref[j * bk:(j + 1) * bk, csl[c]], q_ref[qsl, csl[c]],
                                             (((1,), (1,)), ((), ())), preferred_element_type=F32)

    def consume(j, slot, carry):
        vt = vt_ref[0, j]
        new_ms, new_ls = [], []
        for c in range(2):
            st = s_ref[slot, c]
            m_blk = jnp.max(st, axis=0, keepdims=True)
            if j == 0:
                m_new = m_blk
                p = jnp.exp2(st - m_new)
                new_ls.append(jnp.sum(p, axis=0, keepdims=True))
                acc_ref[c] = jnp.dot(vt, p.astype(BF16), preferred_element_type=F32)
            else:
                ms, ls = carry
                m_new = jnp.maximum(ms[c], m_blk)
                alpha = jnp.exp2(ms[c] - m_new)
                p = jnp.exp2(st - m_new)
                new_ls.append(alpha * ls[c] + jnp.sum(p, axis=0, keepdims=True))
                acc_ref[c] = alpha * acc_ref[c] + jnp.dot(vt, p.astype(BF16), preferred_element_type=F32)
            new_ms.append(m_new)
        return tuple(new_ms), tuple(new_ls)

    def q_block(qi, _):
        carry = None
        for j in range(n_chunks):
            slot = j % 2
            if j + 1 < n_chunks:
                scores(qi, j + 1, 1 - slot)
            else:
                scores(jnp.minimum(qi + 1, n_q - 1), 0, 1 - slot)
            carry = consume(j, slot, carry)
        _, ls = carry
        o = acc_ref[0] * (1.0 / ls[0]) - lam * (acc_ref[1] * (1.0 / ls[1]))
        ms = jnp.mean(o * o, axis=0, keepdims=True)
        y = (o * lax.rsqrt(ms + NORM_EPS)).T * g_ref[...]
        o_ref[pl.ds(pl.multiple_of(qi * bq, bq), bq), :] = (y * lam_ref[1]).astype(o_ref.dtype)
        return 0

    scores(0, 0, 0)
    lax.fori_loop(0, n_q, q_block, 0, unroll=DIFF_Q_UNROLL)


def _diff_attention(qk, vt, lam_consts, lq1, lk1, lq2, lk2, subln_g, batch, seq):
    t = qk.shape[0]
    bq = DIFF_Q_ROWS
    nkb = seq // DIFF_K_ROWS
    vec = lambda n: pl.BlockSpec((1, n), lambda b, h: (0, 0))
    return pl.pallas_call(
        functools.partial(_diff_kernel, seq=seq),
        grid=(batch, DIFF_HEADS),
        in_specs=[pl.BlockSpec(memory_space=pltpu.SMEM),
                  pl.BlockSpec((seq, DIFF_V_DIM), lambda b, h: (b, h)),
                  pl.BlockSpec((seq, DIFF_V_DIM), lambda b, h: (b, DIFF_HEADS + h)),
                  pl.BlockSpec((1, nkb, DIFF_V_DIM, DIFF_K_ROWS), lambda b, h: (b, 0, h, 0)),
                  vec(DIFF_QK_DIM), vec(DIFF_QK_DIM), vec(DIFF_QK_DIM), vec(DIFF_QK_DIM), vec(DIFF_V_DIM)],
        out_specs=pl.BlockSpec((seq, DIFF_V_DIM), lambda b, h: (b, h)),
        out_shape=jax.ShapeDtypeStruct((t, DIFF_WIDTH), BF16),
        scratch_shapes=[pltpu.VMEM((2, DIFF_V_DIM, bq), F32),
                        pltpu.VMEM((2, 2, DIFF_K_ROWS, bq), F32)],
        compiler_params=_params("parallel", "parallel"),
        name="diff_attention",
    )(lam_consts, qk, qk, vt, lq1.reshape(1, -1), lk1.reshape(1, -1), lq2.reshape(1, -1), lk2.reshape(1, -1),
      subln_g.reshape(1, -1))


CONV_CHUNK = 512


def _conv_kernel(u_ref, b_ref, c_ref, w_ref, o_ref, *, seq):
    n_chunks = seq // CONV_CHUNK
    w0, w1, w2 = w_ref[0:1, :], w_ref[1:2, :], w_ref[2:3, :]
    row = lax.broadcasted_iota(jnp.int32, (CONV_CHUNK, LANES), 0)

    def body(ci, carry):
        t0 = pl.multiple_of(ci * CONV_CHUNK, CONV_CHUNK)
        sl = pl.ds(t0, CONV_CHUNK)
        z = c_ref[sl, :] * u_ref[sl, :]
        psl = pl.ds(pl.multiple_of(jnp.maximum(t0 - 8, 0), 8), 8)
        nsl = pl.ds(pl.multiple_of(jnp.minimum(t0 + CONV_CHUNK, seq - 8), 8), 8)
        zp = (c_ref[psl, :] * u_ref[psl, :])[7:8, :]
        zn = (c_ref[nsl, :] * u_ref[nsl, :])[0:1, :]
        zp = jnp.where(ci > 0, zp, 0.0)
        zn = jnp.where(ci < n_chunks - 1, zn, 0.0)
        z_prev = jnp.where(row == 0, zp, pltpu.roll(z, 1, 0))
        z_next = jnp.where(row == CONV_CHUNK - 1, zn, pltpu.roll(z, CONV_CHUNK - 1, 0))
        y = w0 * z_prev + w1 * z + w2 * z_next
        o_ref[sl, :] = (b_ref[sl, :] * y).astype(o_ref.dtype)
        return carry

    lax.fori_loop(0, n_chunks, body, 0)


def _short_conv(ubc, conv_w, batch, seq):
    t = ubc.shape[0]
    nb = CONV_WIDTH // LANES
    return pl.pallas_call(
        functools.partial(_conv_kernel, seq=seq),
        grid=(batch, nb),
        in_specs=[pl.BlockSpec((seq, LANES), lambda b, j: (b, j)),
                  pl.BlockSpec((seq, LANES), lambda b, j: (b, nb + j)),
                  pl.BlockSpec((seq, LANES), lambda b, j: (b, 2 * nb + j)),
                  pl.BlockSpec((3, LANES), lambda b, j: (0, j))],
        out_specs=pl.BlockSpec((seq, LANES), lambda b, j: (b, j)),
        out_shape=jax.ShapeDtypeStruct((t, CONV_WIDTH), BF16),
        compiler_params=_params("parallel", "parallel"),
        name="short_conv",
    )(ubc, ubc, ubc, conv_w)


def _na_col_scale():
    return jnp.concatenate([jnp.full((1, NA_WIDTH), HEAD_DIM ** -0.5, F32), jnp.ones((1, 2 * NA_WIDTH), F32)], axis=1)


def _trunk_layer(x, xg, ssq, layer_idx, batch, seq, rope_tabs, next_norm1_g, weights, cast_srcs, na_bias, lq1, lk1,
                 lq2, lk2, subln_g, conv_w, norm2_g):
    w_in, w_out, w_up, w_down = weights
    nsb = seq // MM_TILE
    na_qkv = _project(xg, ssq, w_in, NA_COL, 3 * NA_WIDTH, "colscale", BF16, extra=_na_col_scale(),
                      extra_spec=pl.BlockSpec((1, MM_TILE), lambda i, j: (0, j)), name="in_proj_na")
    tiles_per_section = DIFF_WIDTH // MM_TILE
    diff_qk = _project(xg, ssq, w_in, DIFF_Q_COL, 2 * DIFF_WIDTH, "rope", BF16, extra=rope_tabs,
                       extra_spec=pl.BlockSpec((None, 3, MM_TILE, DIFF_QK_DIM),
                                               lambda i, j: (j // tiles_per_section, 0, i % nsb, 0)),
                       name="in_proj_qk")
    diff_vt = _project(xg, ssq, w_in, DIFF_V_COL, DIFF_WIDTH, "transpose", BF16, batch=batch, name="in_proj_vt")
    conv_ubc = _project(xg, ssq, w_in, CONV_COL, 3 * CONV_WIDTH, "f32", F32, name="in_proj_conv")

    na_out = _na_attention(na_qkv, na_bias, batch, seq)
    lam_init = 0.8 - 0.6 * math.exp(-0.3 * layer_idx)
    lam_consts = jnp.array([lam_init, 1.0 - lam_init], F32)
    diff_out = _diff_attention(diff_qk, diff_vt, lam_consts, lq1, lk1, lq2, lk2, subln_g, batch, seq)
    conv_out = _short_conv(conv_ubc, conv_w, batch, seq)

    x, xg2, ssq2 = _out_proj([na_out, diff_out, conv_out], w_out, x, norm2_g)
    if cast_srcs and next_norm1_g is not None:
        act, next_weights = _mlp_up(xg2, ssq2, w_up, cast_layer=layer_idx + 1, cast_srcs=cast_srcs)
    else:
        act, next_weights = _mlp_up(xg2, ssq2, w_up)
    if next_norm1_g is None:
        (x,) = _mlp_down(act, w_down, x)
        return (x, None, None), next_weights
    return tuple(_mlp_down(act, w_down, x, g=next_norm1_g)), next_weights


def _run_trunk(x, norm1_g, final_norm_g, layer_params, proj_weights, mlp_weights, cast_srcs=()):
    batch, seq, d = x.shape
    x = x.reshape(batch * seq, d)
    rope_tabs = _rope_tables(seq)
    xg, ssq = _norm_entry(x, norm1_g[0])
    mlp_weights = list(mlp_weights)
    for i, lp in enumerate(layer_params):
        next_g = norm1_g[i + 1] if i + 1 < DEPTH else None
        (x, xg, ssq), next_mlp = _trunk_layer(x, xg, ssq, i, batch, seq, rope_tabs, next_g,
                                              proj_weights[i] + mlp_weights[i], cast_srcs, *lp)
        if next_mlp:
            mlp_weights.append(next_mlp)
    return _rmsnorm(x, final_norm_g, F32).reshape(batch, seq, d), mlp_weights


def kernel(x_prompt, x_sample, norm1_g, w_in, na_rpb, diff_lq1, diff_lk1, diff_lq2, diff_lk2, diff_subln_g,
           conv_w, w_out, norm2_g, w_up, w_down, final_norm_g):
    layer_params = [(_na_bias(na_rpb[i]), diff_lq1[i], diff_lk1[i], diff_lq2[i], diff_lk2[i], diff_subln_g[i],
                     conv_w[i], norm2_g[i]) for i in range(DEPTH)]
    proj_weights = [(w_in[i].astype(BF16), w_out[i].astype(BF16)) for i in range(DEPTH)]
    first_mlp = (w_up[0].astype(BF16), w_down[0].astype(BF16))
    y_prompt, mlp_weights = _run_trunk(x_prompt, norm1_g, final_norm_g, layer_params, proj_weights, [first_mlp],
                                       cast_srcs=(w_up, w_down))
    y_sample, _ = _run_trunk(x_sample, norm1_g, final_norm_g, layer_params, proj_weights, mlp_weights)
    return (y_prompt, y_sample)
```

```python
import functools
import math

import jax
import jax.numpy as jnp
from jax import lax
from jax.experimental import pallas as pl
from jax.experimental.pallas import tpu as pltpu

F32 = jnp.float32
BF16 = jnp.bfloat16

D_MODEL = 4096
DEPTH = 4
GRID_W = 64
HEAD_DIM = 128
NA_HEADS = D_MODEL // 512
NA_WIDTH = NA_HEADS * HEAD_DIM
NA_WIN_ROWS = 8
NA_WIN_COLS = 16
DIFF_HEADS = D_MODEL // 512
DIFF_QK_DIM = HEAD_DIM
DIFF_V_DIM = 2 * DIFF_QK_DIM
DIFF_WIDTH = DIFF_HEADS * DIFF_V_DIM
CONV_WIDTH = D_MODEL - NA_WIDTH - DIFF_WIDTH
IN_WIDTH = 3 * NA_WIDTH + 3 * DIFF_WIDTH + 3 * CONV_WIDTH
D_FF = 4 * D_MODEL
ROPE_THETA = 500000.0
ROT_DIM = DIFF_QK_DIM // 4
NORM_EPS = 1e-5
NEG_INF = -1e30
LOG2_E = 1.4426950408889634

NA_COL = 0
DIFF_Q_COL = 3 * NA_WIDTH
DIFF_K_COL = DIFF_Q_COL + DIFF_WIDTH
DIFF_V_COL = DIFF_K_COL + DIFF_WIDTH
CONV_COL = DIFF_V_COL + DIFF_WIDTH

V7X_VMEM_BYTES = 64 * 1024 * 1024
V7X_VMEM_LIMIT_BYTES = V7X_VMEM_BYTES - 8 * 1024 * 1024
V7X_VMEM_LIMIT_WIDE_BYTES = V7X_VMEM_BYTES - 2 * 1024 * 1024
LANES = 128
MM_TILE = 1024


def _params(*sem, vmem_limit_bytes=V7X_VMEM_LIMIT_BYTES):
    return pltpu.CompilerParams(dimension_semantics=sem, vmem_limit_bytes=vmem_limit_bytes)


def _rmsnorm_kernel(x_ref, g_ref, o_ref):
    x = x_ref[...]
    ms = jnp.mean(x * x, axis=-1, keepdims=True)
    o_ref[...] = ((x * lax.rsqrt(ms + NORM_EPS)) * g_ref[...]).astype(o_ref.dtype)


def _rmsnorm(x, g, out_dtype, block_rows=256):
    t, d = x.shape
    return pl.pallas_call(
        _rmsnorm_kernel,
        grid=(t // block_rows,),
        in_specs=[pl.BlockSpec((block_rows, d), lambda i: (i, 0)),
                  pl.BlockSpec((1, d), lambda i: (0, 0))],
        out_specs=pl.BlockSpec((block_rows, d), lambda i: (i, 0)),
        out_shape=jax.ShapeDtypeStruct((t, d), out_dtype),
        compiler_params=_params("parallel"),
        name="rmsnorm",
    )(x, g.reshape(1, d))


def _norm_entry_kernel(x_ref, g_ref, xg_ref, ssq_ref):
    x = x_ref[...]
    xg_ref[...] = (x * g_ref[...]).astype(BF16)
    ssq_ref[...] = jnp.sum(x * x, axis=-1, keepdims=True)


def _norm_entry(x, g, block_rows=256):
    t, d = x.shape
    return pl.pallas_call(
        _norm_entry_kernel,
        grid=(t // block_rows,),
        in_specs=[pl.BlockSpec((block_rows, d), lambda i: (i, 0)),
                  pl.BlockSpec((1, d), lambda i: (0, 0))],
        out_specs=[pl.BlockSpec((block_rows, d), lambda i: (i, 0)),
                   pl.BlockSpec((block_rows, 1), lambda i: (i, 0))],
        out_shape=[jax.ShapeDtypeStruct((t, d), BF16), jax.ShapeDtypeStruct((t, 1), F32)],
        compiler_params=_params("parallel"),
        name="norm_entry",
    )(x, g.reshape(1, d))


def _row_rscale(ssq_ref):
    return lax.rsqrt(ssq_ref[...] * (1.0 / D_MODEL) + NORM_EPS)


def _rope_lanes(x, c, sa, sb):
    half = ROT_DIM // 2
    return x * c + pltpu.roll(x, DIFF_QK_DIM - half, 1) * sa + pltpu.roll(x, half, 1) * sb


def _proj_kernel(*refs, mode, has_extra, has_cast):
    xg_ref, w_ref, ssq_ref = refs[:3]
    extra_ref = refs[3] if has_extra else None
    o_ref = refs[3 + has_extra + has_cast]
    if has_cast:
        refs[-1][...] = refs[3 + has_extra][...].astype(refs[-1].dtype)
    acc = jnp.dot(xg_ref[...], w_ref[...], preferred_element_type=F32) * _row_rscale(ssq_ref)
    groups = [slice(g * LANES, (g + 1) * LANES) for g in range(acc.shape[1] // LANES)]
    if mode == "f32":
        o_ref[...] = acc
    elif mode == "colscale":
        o_ref[...] = (acc * extra_ref[...]).astype(o_ref.dtype)
    elif mode == "rope":
        c, sa, sb = extra_ref[0], extra_ref[1], extra_ref[2]
        for sl in groups:
            o_ref[:, sl] = _rope_lanes(acc[:, sl], c, sa, sb).astype(o_ref.dtype)
    elif mode == "transpose":
        for sl in groups:
            o_ref[0, 0, sl, :] = acc[:, sl].T.astype(o_ref.dtype)
    else:
        raise ValueError(mode)


def _project(xg, ssq, w, mode, out_dtype, extra=None, extra_spec=None, batch=None, cast_src=None, cast_layer=None,
             cast_col0=0, name="proj"):
    m, k = xg.shape
    n = w.shape[1]
    bm = bn = MM_TILE
    gm = m // bm
    in_specs = [pl.BlockSpec((bm, k), lambda i, j: (i, 0)),
                pl.BlockSpec((k, bn), lambda i, j: (0, j)),
                pl.BlockSpec((bm, 1), lambda i, j: (i, 0))]
    args = [xg, w, ssq]
    if extra is not None:
        in_specs.append(extra_spec)
        args.append(extra)
    if mode == "transpose":
        nsb = m // batch // bm
        out_specs = [pl.BlockSpec((1, 1, bn, bm), lambda i, j: (i // nsb, i % nsb, j, 0))]
        out_shape = [jax.ShapeDtypeStruct((batch, nsb, n, bm), out_dtype)]
    else:
        out_specs = [pl.BlockSpec((bm, bn), lambda i, j: (i, j))]
        out_shape = [jax.ShapeDtypeStruct((m, n), out_dtype)]
    if cast_src is not None:
        assert k % gm == 0 and cast_col0 % bn == 0
        rb = k // gm
        cb0 = cast_col0 // bn
        in_specs.append(pl.BlockSpec((None, rb, bn), lambda i, j: (cast_layer, i, cb0 + j)))
        args.append(cast_src)
        out_specs.append(pl.BlockSpec((rb, bn), lambda i, j: (i, j)))
        out_shape.append(jax.ShapeDtypeStruct((k, n), BF16))
    outs = pl.pallas_call(
        functools.partial(_proj_kernel, mode=mode, has_extra=extra is not None, has_cast=cast_src is not None),
        grid=(gm, n // bn),
        in_specs=in_specs,
        out_specs=out_specs,
        out_shape=out_shape,
        compiler_params=_params("parallel", "arbitrary"),
        name=name,
    )(*args)
    return outs[0], (outs[1] if cast_src is not None else None)


CAST_TILE_ROWS = 512


def _mlp_up_kernel(*refs, n_cast):
    xg_ref, w_ref, ssq_ref = refs[:3]
    src_refs = refs[3:3 + n_cast]
    act_ref = refs[3 + n_cast]
    dst_refs = refs[4 + n_cast:]
    for src_ref, dst_ref in zip(src_refs, dst_refs):
        dst_ref[...] = src_ref[...].astype(dst_ref.dtype)
    acc = jnp.dot(xg_ref[...], w_ref[...], preferred_element_type=F32) * _row_rscale(ssq_ref)
    r = jnp.maximum(acc, 0.0)
    act_ref[...] = (r * r).astype(act_ref.dtype)


def _mlp_up(xg, ssq, w, cast_layer=None, cast_srcs=()):
    m, k = xg.shape
    n = w.shape[1]
    bm = bn = MM_TILE
    gm, gn = m // bm, n // bn
    steps = gm * gn
    tile = pl.BlockSpec((bm, bn), lambda i, j: (i, j))
    in_specs = [pl.BlockSpec((bm, k), lambda i, j: (i, 0)),
                pl.BlockSpec((k, bn), lambda i, j: (0, j)),
                pl.BlockSpec((bm, 1), lambda i, j: (i, 0))]
    out_specs = [tile]
    out_shape = [jax.ShapeDtypeStruct((m, n), BF16)]
    for src in cast_srcs:
        _, rows, cols = src.shape
        rb = CAST_TILE_ROWS
        cb = rows * cols // (steps * rb)
        ct = cols // cb
        assert cb % LANES == 0 and cols % cb == 0 and (rows // rb) * ct == steps
        in_specs.append(pl.BlockSpec((None, rb, cb),
                                     lambda i, j, ct=ct: (cast_layer, (i * gn + j) // ct, (i * gn + j) % ct)))
        out_specs.append(pl.BlockSpec((rb, cb), lambda i, j, ct=ct: ((i * gn + j) // ct, (i * gn + j) % ct)))
        out_shape.append(jax.ShapeDtypeStruct((rows, cols), BF16))
    limit = V7X_VMEM_LIMIT_WIDE_BYTES if cast_srcs else V7X_VMEM_LIMIT_BYTES
    outs = pl.pallas_call(
        functools.partial(_mlp_up_kernel, n_cast=len(cast_srcs)),
        grid=(gm, gn),
        in_specs=in_specs,
        out_specs=out_specs,
        out_shape=out_shape,
        compiler_params=_params("parallel", "arbitrary", vmem_limit_bytes=limit),
        name="mlp_up",
    )(xg, w, ssq, *cast_srcs)
    return outs[0], tuple(outs[1:])


def _emit_norm_inputs(x, g_ref, xg_ref, ssq_ref, first):
    xg_ref[...] = (x * g_ref[...]).astype(BF16)
    part = jnp.sum(x * x, axis=-1, keepdims=True)

    @pl.when(first)
    def _():
        ssq_ref[...] = part

    @pl.when(jnp.logical_not(first))
    def _():
        ssq_ref[...] += part


def _out_proj_kernel(a1_ref, a2_ref, a3_ref, w_ref, res_ref, g_ref, x_ref, xg_ref, ssq_ref, *, k_sizes):
    acc = None
    k0 = 0
    for a_ref, ks in zip((a1_ref, a2_ref, a3_ref), k_sizes):
        d = jnp.dot(a_ref[...], w_ref[k0:k0 + ks, :], preferred_element_type=F32)
        acc = d if acc is None else acc + d
        k0 += ks
    x = acc + res_ref[...]
    x_ref[...] = x
    _emit_norm_inputs(x, g_ref, xg_ref, ssq_ref, pl.program_id(1) == 0)


def _out_proj(a_list, w, res, g):
    m = a_list[0].shape[0]
    k_sizes = tuple(a.shape[1] for a in a_list)
    k, n = w.shape
    assert sum(k_sizes) == k
    bm = bn = MM_TILE
    tile = pl.BlockSpec((bm, bn), lambda i, j: (i, j))
    return pl.pallas_call(
        functools.partial(_out_proj_kernel, k_sizes=k_sizes),
        grid=(m // bm, n // bn),
        in_specs=[pl.BlockSpec((bm, ks), lambda i, j: (i, 0)) for ks in k_sizes]
        + [pl.BlockSpec((k, bn), lambda i, j: (0, j)), tile, pl.BlockSpec((1, bn), lambda i, j: (0, j))],
        out_specs=[tile, tile, pl.BlockSpec((bm, 1), lambda i, j: (i, 0))],
        out_shape=[jax.ShapeDtypeStruct((m, n), F32), jax.ShapeDtypeStruct((m, n), BF16),
                   jax.ShapeDtypeStruct((m, 1), F32)],
        compiler_params=_params("parallel", "arbitrary", vmem_limit_bytes=V7X_VMEM_LIMIT_WIDE_BYTES),
        name="out_proj",
    )(*a_list, w, res, g.reshape(1, n))


def _mlp_down_kernel(*refs, emit_norm):
    if emit_norm:
        a_ref, w_ref, res_ref, g_ref, o_ref, xg_ref, ssq_ref = refs
    else:
        a_ref, w_ref, res_ref, o_ref = refs
    kk = pl.program_id(2)

    @pl.when(kk == 0)
    def _():
        o_ref[...] = res_ref[...]

    o_ref[...] += jnp.dot(a_ref[...], w_ref[...], preferred_element_type=F32)

    if emit_norm:
        @pl.when(kk == pl.num_programs(2) - 1)
        def _():
            _emit_norm_inputs(o_ref[...], g_ref, xg_ref, ssq_ref, pl.program_id(1) == 0)


def _mlp_down(a, w, res, g=None, bm=MM_TILE, bn=MM_TILE, bk=4096):
    m, k = a.shape
    n = w.shape[1]
    emit_norm = g is not None
    tile = pl.BlockSpec((bm, bn), lambda i, j, kk: (i, j))
    in_specs = [pl.BlockSpec((bm, bk), lambda i, j, kk: (i, kk)),
                pl.BlockSpec((bk, bn), lambda i, j, kk: (kk, j)),
                tile]
    args = [a, w, res]
    out_specs = [tile]
    out_shape = [jax.ShapeDtypeStruct((m, n), F32)]
    if emit_norm:
        in_specs.append(pl.BlockSpec((1, bn), lambda i, j, kk: (0, j)))
        args.append(g.reshape(1, n))
        out_specs += [tile, pl.BlockSpec((bm, 1), lambda i, j, kk: (i, 0))]
        out_shape += [jax.ShapeDtypeStruct((m, n), BF16), jax.ShapeDtypeStruct((m, 1), F32)]
    return pl.pallas_call(
        functools.partial(_mlp_down_kernel, emit_norm=emit_norm),
        grid=(m // bm, n // bn, k // bk),
        in_specs=in_specs,
        out_specs=out_specs,
        out_shape=out_shape,
        compiler_params=_params("parallel", "arbitrary", "arbitrary", vmem_limit_bytes=V7X_VMEM_LIMIT_WIDE_BYTES),
        name="mlp_down",
    )(*args)


NA_BIAS_VARIANTS = NA_WIN_ROWS
NA_BAND_KEYS = NA_WIN_ROWS * GRID_W
RPB_ROWS = 2 * NA_WIN_ROWS - 1
RPB_COLS = 2 * NA_WIN_COLS - 1
NA_ROW_GROUP = 16


def _na_bias_kernel(rpb_ref, o_ref):
    h = pl.program_id(0)
    shape = (GRID_W, 2 * GRID_W)
    lane = lax.broadcasted_iota(jnp.int32, shape, 1)
    cq = lax.broadcasted_iota(jnp.int32, shape, 0)
    ck = lane & (GRID_W - 1)
    upper = lane >= GRID_W
    dcl = jnp.clip(ck - cq, -(NA_WIN_COLS - 1), NA_WIN_COLS - 1) + (NA_WIN_COLS - 1)
    c0 = jnp.clip(cq - NA_WIN_COLS // 2, 0, GRID_W - NA_WIN_COLS)
    col_in = jnp.logical_and(ck >= c0, ck < c0 + NA_WIN_COLS)
    pairs = []
    for d in range(RPB_ROWS - 1):
        acc = jnp.zeros(shape, F32)
        for dc in range(RPB_COLS):
            lo = rpb_ref[(h * RPB_ROWS + d) * RPB_COLS + dc]
            hi = rpb_ref[(h * RPB_ROWS + d + 1) * RPB_COLS + dc]
            acc = jnp.where(dcl == dc, jnp.where(upper, hi, lo), acc)
        pairs.append(jnp.where(col_in, acc, NEG_INF))
    for off in range(NA_BIAS_VARIANTS):
        for jj in range(NA_WIN_ROWS // 2):
            o_ref[0, off, :, jj * 2 * GRID_W:(jj + 1) * 2 * GRID_W] = pairs[off + 2 * jj]


def _na_bias(rpb):
    return pl.pallas_call(
        _na_bias_kernel,
        grid=(NA_HEADS,),
        in_specs=[pl.BlockSpec(memory_space=pltpu.SMEM)],
        out_specs=pl.BlockSpec((1, NA_BIAS_VARIANTS, GRID_W, NA_BAND_KEYS), lambda h: (h, 0, 0, 0)),
        out_shape=jax.ShapeDtypeStruct((NA_HEADS, NA_BIAS_VARIANTS, GRID_W, NA_BAND_KEYS), F32),
        compiler_params=_params("arbitrary"),
        name="na_bias",
    )(rpb.reshape(-1))


def _na_kernel(q_ref, k_ref, v_ref, bias_ref, o_ref, *, seq):
    rows = seq // GRID_W

    def group_body(gi, carry):
        qsls, ksls, scores = [], [], []
        for u in range(NA_ROW_GROUP):
            r = gi * NA_ROW_GROUP + u
            r0 = jnp.clip(r - NA_WIN_ROWS // 2, 0, rows - NA_WIN_ROWS)
            off = r0 - r + (NA_WIN_ROWS - 1)
            qsl = pl.ds(pl.multiple_of(r * GRID_W, GRID_W), GRID_W)
            ksl = pl.ds(pl.multiple_of(r0 * GRID_W, GRID_W), NA_BAND_KEYS)
            s = lax.dot_general(q_ref[qsl, :], k_ref[ksl, :], (((1,), (1,)), ((), ())),
                                preferred_element_type=F32)
            scores.append(s + bias_ref[0, off])
            qsls.append(qsl)
            ksls.append(ksl)
        probs, inv_ls = [], []
        for s in scores:
            p = jnp.exp(s - jnp.max(s, axis=-1, keepdims=True))
            inv_ls.append(1.0 / jnp.sum(p, axis=-1, keepdims=True))
            probs.append(p.astype(BF16))
        for qsl, ksl, p, inv_l in zip(qsls, ksls, probs, inv_ls):
            o = jnp.dot(p, v_ref[ksl, :], preferred_element_type=F32)
            o_ref[qsl, :] = (o * inv_l).astype(o_ref.dtype)
        return carry

    lax.fori_loop(0, rows // NA_ROW_GROUP, group_body, 0)


def _na_attention(qkv, bias, batch, seq):
    t = qkv.shape[0]
    return pl.pallas_call(
        functools.partial(_na_kernel, seq=seq),
        grid=(batch, NA_HEADS),
        in_specs=[pl.BlockSpec((seq, HEAD_DIM), lambda b, h: (b, h)),
                  pl.BlockSpec((seq, HEAD_DIM), lambda b, h: (b, NA_HEADS + h)),
                  pl.BlockSpec((seq, HEAD_DIM), lambda b, h: (b, 2 * NA_HEADS + h)),
                  pl.BlockSpec((1, NA_BIAS_VARIANTS, GRID_W, NA_BAND_KEYS), lambda b, h: (h, 0, 0, 0))],
        out_specs=pl.BlockSpec((seq, HEAD_DIM), lambda b, h: (b, h)),
        out_shape=jax.ShapeDtypeStruct((t, NA_WIDTH), BF16),
        compiler_params=_params("parallel", "parallel"),
        name="na_attention",
    )(qkv, qkv, qkv, bias)


DIFF_K_ROWS = MM_TILE
DIFF_Q_ROWS = 256
DIFF_Q_UNROLL = 4


def _rope_tables(seq):
    half = ROT_DIM // 2
    inv_freq = ROPE_THETA ** (-jnp.arange(half, dtype=F32) / half)
    ang = jnp.arange(seq, dtype=F32)[:, None] * inv_freq[None, :]
    cos, sin = jnp.cos(ang), jnp.sin(ang)
    rest = DIFF_QK_DIM - ROT_DIM
    c = jnp.concatenate([cos, cos, jnp.ones((seq, rest), F32)], axis=1)
    sa = jnp.concatenate([-sin, jnp.zeros((seq, DIFF_QK_DIM - half), F32)], axis=1)
    sb = jnp.concatenate([jnp.zeros((seq, half), F32), sin, jnp.zeros((seq, rest), F32)], axis=1)
    k_tabs = jnp.stack([c, sa, sb])
    return jnp.stack([k_tabs * (DIFF_QK_DIM ** -0.5 * LOG2_E), k_tabs])


def _diff_kernel(lam_ref, q_ref, k_ref, vt_ref, lq1_ref, lk1_ref, lq2_ref, lk2_ref, g_ref, o_ref, acc_ref, s_ref,
                 *, seq):
    bk = DIFF_K_ROWS
    bq = DIFF_Q_ROWS
    n_chunks = seq // bk
    n_q = seq // bq
    assert n_chunks % 2 == 0
    csl = [slice(c * DIFF_QK_DIM, (c + 1) * DIFF_QK_DIM) for c in range(2)]
    lam_init = lam_ref[0]
    lam = (jnp.exp(jnp.sum(lq1_ref[...] * lk1_ref[...], axis=-1, keepdims=True))
           - jnp.exp(jnp.sum(lq2_ref[...] * lk2_ref[...], axis=-1, keepdims=True)) + lam_init)

    def scores(qi, j, slot):
        qsl = pl.ds(pl.multiple_of(qi * bq, bq), bq)
        for c in range(2):
            s_ref[slot, c] = lax.dot_general(k_ref[j * bk:(j + 1) * bk, csl[c]], q_ref[qsl, csl[c]],
                                             (((1,), (1,)), ((), ())), preferred_element_type=F32)

    def consume(j, slot, carry):
        vt = vt_ref[0, j]
        new_ms, new_ls = [], []
        for c in range(2):
            st = s_ref[slot, c]
            m_blk = jnp.max(st, axis=0, keepdims=True)
            if j == 0:
                m_new = m_blk
                p = jnp.exp2(st - m_new)
                new_ls.append(jnp.sum(p, axis=0, keepdims=True))
                acc_ref[c] = jnp.dot(vt, p.astype(BF16), preferred_element_type=F32)
            else:
                ms, ls = carry
                m_new = jnp.maximum(ms[c], m_blk)
                alpha = jnp.exp2(ms[c] - m_new)
                p = jnp.exp2(st - m_new)
                new_ls.append(alpha * ls[c] + jnp.sum(p, axis=0, keepdims=True))
                acc_ref[c] = alpha * acc_ref[c] + jnp.dot(vt, p.astype(BF16), preferred_element_type=F32)
            new_ms.append(m_new)
        return tuple(new_ms), tuple(new_ls)

    def q_block(qi, _):
        carry = None
        for j in range(n_chunks):
            slot = j % 2
            if j + 1 < n_chunks:
                scores(qi, j + 1, 1 - slot)
            else:
                scores(jnp.minimum(qi + 1, n_q - 1), 0, 1 - slot)
            carry = consume(j, slot, carry)
        _, ls = carry
        o = acc_ref[0] * (1.0 / ls[0]) - lam * (acc_ref[1] * (1.0 / ls[1]))
        ms = jnp.mean(o * o, axis=0, keepdims=True)
        y = (o * lax.rsqrt(ms + NORM_EPS)).T * g_ref[...]
        o_ref[pl.ds(pl.multiple_of(qi * bq, bq), bq), :] = (y * lam_ref[1]).astype(o_ref.dtype)
        return 0

    scores(0, 0, 0)
    lax.fori_loop(0, n_q, q_block, 0, unroll=DIFF_Q_UNROLL)


def _diff_attention(qk, vt, lam_consts, lq1, lk1, lq2, lk2, subln_g, batch, seq):
    t = qk.shape[0]
    bq = DIFF_Q_ROWS
    nkb = seq // DIFF_K_ROWS
    vec = lambda n: pl.BlockSpec((1, n), lambda b, h: (0, 0))
    return pl.pallas_call(
        functools.partial(_diff_kernel, seq=seq),
        grid=(batch, DIFF_HEADS),
        in_specs=[pl.BlockSpec(memory_space=pltpu.SMEM),
                  pl.BlockSpec((seq, DIFF_V_DIM), lambda b, h: (b, h)),
                  pl.BlockSpec((seq, DIFF_V_DIM), lambda b, h: (b, DIFF_HEADS + h)),
                  pl.BlockSpec((1, nkb, DIFF_V_DIM, DIFF_K_ROWS), lambda b, h: (b, 0, h, 0)),
                  vec(DIFF_QK_DIM), vec(DIFF_QK_DIM), vec(DIFF_QK_DIM), vec(DIFF_QK_DIM), vec(DIFF_V_DIM)],
        out_specs=pl.BlockSpec((seq, DIFF_V_DIM), lambda b, h: (b, h)),
        out_shape=jax.ShapeDtypeStruct((t, DIFF_WIDTH), BF16),
        scratch_shapes=[pltpu.VMEM((2, DIFF_V_DIM, bq), F32),
                        pltpu.VMEM((2, 2, DIFF_K_ROWS, bq), F32)],
        compiler_params=_params("parallel", "parallel"),
        name="diff_attention",
    )(lam_consts, qk, qk, vt, lq1.reshape(1, -1), lk1.reshape(1, -1), lq2.reshape(1, -1), lk2.reshape(1, -1),
      subln_g.reshape(1, -1))


CONV_CHUNK = 512


def _conv_kernel(u_ref, b_ref, c_ref, w_ref, o_ref, *, seq):
    n_chunks = seq // CONV_CHUNK
    w0, w1, w2 = w_ref[0:1, :], w_ref[1:2, :], w_ref[2:3, :]
    row = lax.broadcasted_iota(jnp.int32, (CONV_CHUNK, LANES), 0)

    def body(ci, carry):
        t0 = pl.multiple_of(ci * CONV_CHUNK, CONV_CHUNK)
        sl = pl.ds(t0, CONV_CHUNK)
        z = c_ref[sl, :] * u_ref[sl, :]
        psl = pl.ds(pl.multiple_of(jnp.maximum(t0 - 8, 0), 8), 8)
        nsl = pl.ds(pl.multiple_of(jnp.minimum(t0 + CONV_CHUNK, seq - 8), 8), 8)
        zp = (c_ref[psl, :] * u_ref[psl, :])[7:8, :]
        zn = (c_ref[nsl, :] * u_ref[nsl, :])[0:1, :]
        zp = jnp.where(ci > 0, zp, 0.0)
        zn = jnp.where(ci < n_chunks - 1, zn, 0.0)
        z_prev = jnp.where(row == 0, zp, pltpu.roll(z, 1, 0))
        z_next = jnp.where(row == CONV_CHUNK - 1, zn, pltpu.roll(z, CONV_CHUNK - 1, 0))
        y = w0 * z_prev + w1 * z + w2 * z_next
        o_ref[sl, :] = (b_ref[sl, :] * y).astype(o_ref.dtype)
        return carry

    lax.fori_loop(0, n_chunks, body, 0)


def _short_conv(ubc, conv_w, batch, seq):
    t = ubc.shape[0]
    nb = CONV_WIDTH // LANES
    return pl.pallas_call(
        functools.partial(_conv_kernel, seq=seq),
        grid=(batch, nb),
        in_specs=[pl.BlockSpec((seq, LANES), lambda b, j: (b, j)),
                  pl.BlockSpec((seq, LANES), lambda b, j: (b, nb + j)),
                  pl.BlockSpec((seq, LANES), lambda b, j: (b, 2 * nb + j)),
                  pl.BlockSpec((3, LANES), lambda b, j: (0, j))],
        out_specs=pl.BlockSpec((seq, LANES), lambda b, j: (b, j)),
        out_shape=jax.ShapeDtypeStruct((t, CONV_WIDTH), BF16),
        compiler_params=_params("parallel", "parallel"),
        name="short_conv",
    )(ubc, ubc, ubc, conv_w)


def _na_col_scale():
    return jnp.concatenate([jnp.full((1, NA_WIDTH), HEAD_DIM ** -0.5, F32), jnp.ones((1, 2 * NA_WIDTH), F32)], axis=1)


IN_SECTIONS = ((NA_COL, 3 * NA_WIDTH), (DIFF_Q_COL, 2 * DIFF_WIDTH), (DIFF_V_COL, DIFF_WIDTH), (CONV_COL, 3 * CONV_WIDTH))


def _trunk_layer(x, xg, ssq, layer_idx, batch, seq, rope_tabs, next_norm1_g, weights, cast_srcs, na_bias, lq1, lk1,
                 lq2, lk2, subln_g, conv_w, norm2_g):
    w_in_sections, w_out, w_up, w_down = weights
    do_cast = bool(cast_srcs) and next_norm1_g is not None
    nsb = seq // MM_TILE
    tiles_per_qk = DIFF_WIDTH // MM_TILE
    modes = (("colscale", BF16, _na_col_scale(), pl.BlockSpec((1, MM_TILE), lambda i, j: (0, j)), "in_proj_na"),
             ("rope", BF16, rope_tabs, pl.BlockSpec((None, 3, MM_TILE, DIFF_QK_DIM),
                                                   lambda i, j: (j // tiles_per_qk, 0, i % nsb, 0)), "in_proj_qk"),
             ("transpose", BF16, None, None, "in_proj_vt"),
             ("f32", F32, None, None, "in_proj_conv"))
    projected, next_in_sections = [], []
    for w_sec, (col0, _), (mode, dtype, extra, extra_spec, name) in zip(w_in_sections, IN_SECTIONS, modes):
        cast = dict(cast_src=cast_srcs[0], cast_layer=layer_idx + 1, cast_col0=col0) if do_cast else {}
        out, w_next = _project(xg, ssq, w_sec, mode, dtype, extra=extra, extra_spec=extra_spec, batch=batch,
                               name=name, **cast)
        projected.append(out)
        next_in_sections.append(w_next)
    na_qkv, diff_qk, diff_vt, conv_ubc = projected

    na_out = _na_attention(na_qkv, na_bias, batch, seq)
    lam_init = 0.8 - 0.6 * math.exp(-0.3 * layer_idx)
    lam_consts = jnp.array([lam_init, 1.0 - lam_init], F32)
    diff_out = _diff_attention(diff_qk, diff_vt, lam_consts, lq1, lk1, lq2, lk2, subln_g, batch, seq)
    conv_out = _short_conv(conv_ubc, conv_w, batch, seq)

    x, xg2, ssq2 = _out_proj([na_out, diff_out, conv_out], w_out, x, norm2_g)
    if do_cast:
        act, next_rest = _mlp_up(xg2, ssq2, w_up, cast_layer=layer_idx + 1, cast_srcs=cast_srcs[1:])
        next_weights = (tuple(next_in_sections),) + next_rest
    else:
        act, _ = _mlp_up(xg2, ssq2, w_up)
        next_weights = None
    if next_norm1_g is None:
        (x,) = _mlp_down(act, w_down, x)
        return (x, None, None), next_weights
    return tuple(_mlp_down(act, w_down, x, g=next_norm1_g)), next_weights


def _run_trunk(x, norm1_g, final_norm_g, layer_params, layer_weights, cast_srcs=()):
    batch, seq, d = x.shape
    x = x.reshape(batch * seq, d)
    rope_tabs = _rope_tables(seq)
    xg, ssq = _norm_entry(x, norm1_g[0])
    layer_weights = list(layer_weights)
    for i, lp in enumerate(layer_params):
        next_g = norm1_g[i + 1] if i + 1 < DEPTH else None
        (x, xg, ssq), next_weights = _trunk_layer(x, xg, ssq, i, batch, seq, rope_tabs, next_g, layer_weights[i],
                                                  cast_srcs, *lp)
        if next_weights is not None:
            layer_weights.append(next_weights)
    return _rmsnorm(x, final_norm_g, F32).reshape(batch, seq, d), layer_weights


def kernel(x_prompt, x_sample, norm1_g, w_in, na_rpb, diff_lq1, diff_lk1, diff_lq2, diff_lk2, diff_subln_g,
           conv_w, w_out, norm2_g, w_up, w_down, final_norm_g):
    layer_params = [(_na_bias(na_rpb[i]), diff_lq1[i], diff_lk1[i], diff_lq2[i], diff_lk2[i], diff_subln_g[i],
                     conv_w[i], norm2_g[i]) for i in range(DEPTH)]
    first_weights = (tuple(w_in[0, :, c0:c0 + width].astype(BF16) for c0, width in IN_SECTIONS),
                     w_out[0].astype(BF16), w_up[0].astype(BF16), w_down[0].astype(BF16))
    y_prompt, layer_weights = _run_trunk(x_prompt, norm1_g, final_norm_g, layer_params, [first_weights],
                                         cast_srcs=(w_in, w_out, w_up, w_down))
    y_sample, _ = _run_trunk(x_sample, norm1_g, final_norm_g, layer_params, layer_weights)
    return (y_prompt, y_sample)
```

```python
import functools
import math

import jax
import jax.numpy as jnp
from jax import lax
from jax.experimental import pallas as pl
from jax.experimental.pallas import tpu as pltpu

F32 = jnp.float32
BF16 = jnp.bfloat16

D_MODEL = 4096
DEPTH = 4
GRID_W = 64
HEAD_DIM = 128
NA_HEADS = D_MODEL // 512
NA_WIDTH = NA_HEADS * HEAD_DIM
NA_WIN_ROWS = 8
NA_WIN_COLS = 16
DIFF_HEADS = D_MODEL // 512
DIFF_QK_DIM = HEAD_DIM
DIFF_V_DIM = 2 * DIFF_QK_DIM
DIFF_WIDTH = DIFF_HEADS * DIFF_V_DIM
CONV_WIDTH = D_MODEL - NA_WIDTH - DIFF_WIDTH
IN_WIDTH = 3 * NA_WIDTH + 3 * DIFF_WIDTH + 3 * CONV_WIDTH
D_FF = 4 * D_MODEL
ROPE_THETA = 500000.0
ROT_DIM = DIFF_QK_DIM // 4
NORM_EPS = 1e-5
NEG_INF = -1e30
LOG2_E = 1.4426950408889634

NA_COL = 0
DIFF_Q_COL = 3 * NA_WIDTH
DIFF_K_COL = DIFF_Q_COL + DIFF_WIDTH
DIFF_V_COL = DIFF_K_COL + DIFF_WIDTH
CONV_COL = DIFF_V_COL + DIFF_WIDTH

V7X_VMEM_BYTES = 64 * 1024 * 1024
V7X_VMEM_LIMIT_BYTES = V7X_VMEM_BYTES - 8 * 1024 * 1024
V7X_VMEM_LIMIT_WIDE_BYTES = V7X_VMEM_BYTES - 2 * 1024 * 1024
LANES = 128
MM_TILE = 1024


def _params(*sem, vmem_limit_bytes=V7X_VMEM_LIMIT_BYTES):
    return pltpu.CompilerParams(dimension_semantics=sem, vmem_limit_bytes=vmem_limit_bytes)


def _rmsnorm_kernel(x_ref, g_ref, o_ref):
    x = x_ref[...]
    ms = jnp.mean(x * x, axis=-1, keepdims=True)
    o_ref[...] = ((x * lax.rsqrt(ms + NORM_EPS)) * g_ref[...]).astype(o_ref.dtype)


def _rmsnorm(x, g, out_dtype, block_rows=256):
    t, d = x.shape
    return pl.pallas_call(
        _rmsnorm_kernel,
        grid=(t // block_rows,),
        in_specs=[pl.BlockSpec((block_rows, d), lambda i: (i, 0)),
                  pl.BlockSpec((1, d), lambda i: (0, 0))],
        out_specs=pl.BlockSpec((block_rows, d), lambda i: (i, 0)),
        out_shape=jax.ShapeDtypeStruct((t, d), out_dtype),
        compiler_params=_params("parallel"),
        name="rmsnorm",
    )(x, g.reshape(1, d))


def _norm_entry_kernel(x_ref, g_ref, xg_ref, ssq_ref):
    x = x_ref[...]
    xg_ref[...] = (x * g_ref[...]).astype(BF16)
    ssq_ref[...] = jnp.sum(x * x, axis=-1, keepdims=True)


def _norm_entry(x, g, block_rows=256):
    t, d = x.shape
    return pl.pallas_call(
        _norm_entry_kernel,
        grid=(t // block_rows,),
        in_specs=[pl.BlockSpec((block_rows, d), lambda i: (i, 0)),
                  pl.BlockSpec((1, d), lambda i: (0, 0))],
        out_specs=[pl.BlockSpec((block_rows, d), lambda i: (i, 0)),
                   pl.BlockSpec((block_rows, 1), lambda i: (i, 0))],
        out_shape=[jax.ShapeDtypeStruct((t, d), BF16), jax.ShapeDtypeStruct((t, 1), F32)],
        compiler_params=_params("parallel"),
        name="norm_entry",
    )(x, g.reshape(1, d))


def _row_rscale(ssq_ref):
    return lax.rsqrt(ssq_ref[...] * (1.0 / D_MODEL) + NORM_EPS)


def _rope_lanes(x, c, sa, sb):
    half = ROT_DIM // 2
    return x * c + pltpu.roll(x, DIFF_QK_DIM - half, 1) * sa + pltpu.roll(x, half, 1) * sb


def _proj_kernel(*refs, mode, has_extra, has_cast):
    xg_ref, w_ref, ssq_ref = refs[:3]
    extra_ref = refs[3] if has_extra else None
    o_ref = refs[3 + has_extra + has_cast]
    if has_cast:
        refs[-1][...] = refs[3 + has_extra][...].astype(refs[-1].dtype)
    acc = jnp.dot(xg_ref[...], w_ref[...], preferred_element_type=F32) * _row_rscale(ssq_ref)
    groups = [slice(g * LANES, (g + 1) * LANES) for g in range(acc.shape[1] // LANES)]
    if mode == "f32":
        o_ref[...] = acc
    elif mode == "colscale":
        o_ref[...] = (acc * extra_ref[...]).astype(o_ref.dtype)
    elif mode == "rope":
        c, sa, sb = extra_ref[0], extra_ref[1], extra_ref[2]
        for sl in groups:
            o_ref[:, sl] = _rope_lanes(acc[:, sl], c, sa, sb).astype(o_ref.dtype)
    elif mode == "transpose":
        for sl in groups:
            o_ref[0, 0, sl, :] = acc[:, sl].T.astype(o_ref.dtype)
    else:
        raise ValueError(mode)


def _project(xg, ssq, w, mode, out_dtype, extra=None, extra_spec=None, batch=None, cast_src=None, cast_layer=None,
             cast_col0=0, name="proj"):
    m, k = xg.shape
    n = w.shape[1]
    bm = bn = MM_TILE
    gm = m // bm
    in_specs = [pl.BlockSpec((bm, k), lambda i, j: (i, 0)),
                pl.BlockSpec((k, bn), lambda i, j: (0, j)),
                pl.BlockSpec((bm, 1), lambda i, j: (i, 0))]
    args = [xg, w, ssq]
    if extra is not None:
        in_specs.append(extra_spec)
        args.append(extra)
    if mode == "transpose":
        nsb = m // batch // bm
        out_specs = [pl.BlockSpec((1, 1, bn, bm), lambda i, j: (i // nsb, i % nsb, j, 0))]
        out_shape = [jax.ShapeDtypeStruct((batch, nsb, n, bm), out_dtype)]
    else:
        out_specs = [pl.BlockSpec((bm, bn), lambda i, j: (i, j))]
        out_shape = [jax.ShapeDtypeStruct((m, n), out_dtype)]
    if cast_src is not None:
        assert k % gm == 0 and cast_col0 % bn == 0
        rb = k // gm
        cb0 = cast_col0 // bn
        in_specs.append(pl.BlockSpec((None, rb, bn), lambda i, j: (cast_layer, i, cb0 + j)))
        args.append(cast_src)
        out_specs.append(pl.BlockSpec((rb, bn), lambda i, j: (i, j)))
        out_shape.append(jax.ShapeDtypeStruct((k, n), BF16))
    outs = pl.pallas_call(
        functools.partial(_proj_kernel, mode=mode, has_extra=extra is not None, has_cast=cast_src is not None),
        grid=(gm, n // bn),
        in_specs=in_specs,
        out_specs=out_specs,
        out_shape=out_shape,
        compiler_params=_params("parallel", "arbitrary"),
        name=name,
    )(*args)
    return outs[0], (outs[1] if cast_src is not None else None)


CAST_TILE_ROWS = 512


def _mlp_up_kernel(*refs, n_cast):
    xg_ref, w_ref, ssq_ref = refs[:3]
    src_refs = refs[3:3 + n_cast]
    act_ref = refs[3 + n_cast]
    dst_refs = refs[4 + n_cast:]
    for src_ref, dst_ref in zip(src_refs, dst_refs):
        dst_ref[...] = src_ref[...].astype(dst_ref.dtype)
    acc = jnp.dot(xg_ref[...], w_ref[...], preferred_element_type=F32) * _row_rscale(ssq_ref)
    r = jnp.maximum(acc, 0.0)
    act_ref[...] = (r * r).astype(act_ref.dtype)


def _mlp_up(xg, ssq, w, cast_layer=None, cast_srcs=()):
    m, k = xg.shape
    n = w.shape[1]
    bm = bn = MM_TILE
    gm, gn = m // bm, n // bn
    steps = gm * gn
    tile = pl.BlockSpec((bm, bn), lambda i, j: (i, j))
    in_specs = [pl.BlockSpec((bm, k), lambda i, j: (i, 0)),
                pl.BlockSpec((k, bn), lambda i, j: (0, j)),
                pl.BlockSpec((bm, 1), lambda i, j: (i, 0))]
    out_specs = [tile]
    out_shape = [jax.ShapeDtypeStruct((m, n), BF16)]
    for src in cast_srcs:
        _, rows, cols = src.shape
        rb = CAST_TILE_ROWS
        cb = rows * cols // (steps * rb)
        ct = cols // cb
        assert cb % LANES == 0 and cols % cb == 0 and (rows // rb) * ct == steps
        in_specs.append(pl.BlockSpec((None, rb, cb),
                                     lambda i, j, ct=ct: (cast_layer, (i * gn + j) // ct, (i * gn + j) % ct)))
        out_specs.append(pl.BlockSpec((rb, cb), lambda i, j, ct=ct: ((i * gn + j) // ct, (i * gn + j) % ct)))
        out_shape.append(jax.ShapeDtypeStruct((rows, cols), BF16))
    limit = V7X_VMEM_LIMIT_WIDE_BYTES if cast_srcs else V7X_VMEM_LIMIT_BYTES
    outs = pl.pallas_call(
        functools.partial(_mlp_up_kernel, n_cast=len(cast_srcs)),
        grid=(gm, gn),
        in_specs=in_specs,
        out_specs=out_specs,
        out_shape=out_shape,
        compiler_params=_params("parallel", "arbitrary", vmem_limit_bytes=limit),
        name="mlp_up",
    )(xg, w, ssq, *cast_srcs)
    return outs[0], tuple(outs[1:])


def _emit_norm_inputs(x, g_ref, xg_ref, ssq_ref, first):
    xg_ref[...] = (x * g_ref[...]).astype(BF16)
    part = jnp.sum(x * x, axis=-1, keepdims=True)

    @pl.when(first)
    def _():
        ssq_ref[...] = part

    @pl.when(jnp.logical_not(first))
    def _():
        ssq_ref[...] += part


def _out_proj_kernel(a1_ref, a2_ref, a3_ref, w_ref, res_ref, g_ref, x_ref, xg_ref, ssq_ref, *, k_sizes):
    acc = None
    k0 = 0
    for a_ref, ks in zip((a1_ref, a2_ref, a3_ref), k_sizes):
        d = jnp.dot(a_ref[...], w_ref[k0:k0 + ks, :], preferred_element_type=F32)
        acc = d if acc is None else acc + d
        k0 += ks
    x = acc + res_ref[...]
    x_ref[...] = x
    _emit_norm_inputs(x, g_ref, xg_ref, ssq_ref, pl.program_id(1) == 0)


def _out_proj(a_list, w, res, g):
    m = a_list[0].shape[0]
    k_sizes = tuple(a.shape[1] for a in a_list)
    k, n = w.shape
    assert sum(k_sizes) == k
    bm = bn = MM_TILE
    tile = pl.BlockSpec((bm, bn), lambda i, j: (i, j))
    return pl.pallas_call(
        functools.partial(_out_proj_kernel, k_sizes=k_sizes),
        grid=(m // bm, n // bn),
        in_specs=[pl.BlockSpec((bm, ks), lambda i, j: (i, 0)) for ks in k_sizes]
        + [pl.BlockSpec((k, bn), lambda i, j: (0, j)), tile, pl.BlockSpec((1, bn), lambda i, j: (0, j))],
        out_specs=[tile, tile, pl.BlockSpec((bm, 1), lambda i, j: (i, 0))],
        out_shape=[jax.ShapeDtypeStruct((m, n), F32), jax.ShapeDtypeStruct((m, n), BF16),
                   jax.ShapeDtypeStruct((m, 1), F32)],
        compiler_params=_params("parallel", "arbitrary", vmem_limit_bytes=V7X_VMEM_LIMIT_WIDE_BYTES),
        name="out_proj",
    )(*a_list, w, res, g.reshape(1, n))


def _mlp_down_kernel(*refs, emit_norm):
    if emit_norm:
        a_ref, w_ref, res_ref, g_ref, o_ref, xg_ref, ssq_ref = refs
    else:
        a_ref, w_ref, res_ref, o_ref = refs
    kk = pl.program_id(2)

    @pl.when(kk == 0)
    def _():
        o_ref[...] = res_ref[...]

    o_ref[...] += jnp.dot(a_ref[...], w_ref[...], preferred_element_type=F32)

    if emit_norm:
        @pl.when(kk == pl.num_programs(2) - 1)
        def _():
            _emit_norm_inputs(o_ref[...], g_ref, xg_ref, ssq_ref, pl.program_id(1) == 0)


def _mlp_down(a, w, res, g=None, bm=MM_TILE, bn=MM_TILE, bk=4096):
    m, k = a.shape
    n = w.shape[1]
    emit_norm = g is not None
    tile = pl.BlockSpec((bm, bn), lambda i, j, kk: (i, j))
    in_specs = [pl.BlockSpec((bm, bk), lambda i, j, kk: (i, kk)),
                pl.BlockSpec((bk, bn), lambda i, j, kk: (kk, j)),
                tile]
    args = [a, w, res]
    out_specs = [tile]
    out_shape = [jax.ShapeDtypeStruct((m, n), F32)]
    if emit_norm:
        in_specs.append(pl.BlockSpec((1, bn), lambda i, j, kk: (0, j)))
        args.append(g.reshape(1, n))
        out_specs += [tile, pl.BlockSpec((bm, 1), lambda i, j, kk: (i, 0))]
        out_shape += [jax.ShapeDtypeStruct((m, n), BF16), jax.ShapeDtypeStruct((m, 1), F32)]
    return pl.pallas_call(
        functools.partial(_mlp_down_kernel, emit_norm=emit_norm),
        grid=(m // bm, n // bn, k // bk),
        in_specs=in_specs,
        out_specs=out_specs,
        out_shape=out_shape,
        compiler_params=_params("parallel", "arbitrary", "arbitrary", vmem_limit_bytes=V7X_VMEM_LIMIT_WIDE_BYTES),
        name="mlp_down",
    )(*args)


NA_BIAS_VARIANTS = NA_WIN_ROWS
NA_BAND_KEYS = NA_WIN_ROWS * GRID_W
RPB_ROWS = 2 * NA_WIN_ROWS - 1
RPB_COLS = 2 * NA_WIN_COLS - 1
NA_ROW_GROUP = 16


def _na_bias_kernel(rpb_ref, o_ref):
    h = pl.program_id(0)
    shape = (GRID_W, 2 * GRID_W)
    lane = lax.broadcasted_iota(jnp.int32, shape, 1)
    cq = lax.broadcasted_iota(jnp.int32, shape, 0)
    ck = lane & (GRID_W - 1)
    upper = lane >= GRID_W
    dcl = jnp.clip(ck - cq, -(NA_WIN_COLS - 1), NA_WIN_COLS - 1) + (NA_WIN_COLS - 1)
    c0 = jnp.clip(cq - NA_WIN_COLS // 2, 0, GRID_W - NA_WIN_COLS)
    col_in = jnp.logical_and(ck >= c0, ck < c0 + NA_WIN_COLS)
    pairs = []
    for d in range(RPB_ROWS - 1):
        acc = jnp.zeros(shape, F32)
        for dc in range(RPB_COLS):
            lo = rpb_ref[(h * RPB_ROWS + d) * RPB_COLS + dc]
            hi = rpb_ref[(h * RPB_ROWS + d + 1) * RPB_COLS + dc]
            acc = jnp.where(dcl == dc, jnp.where(upper, hi, lo), acc)
        pairs.append(jnp.where(col_in, acc, NEG_INF))
    for off in range(NA_BIAS_VARIANTS):
        for jj in range(NA_WIN_ROWS // 2):
            o_ref[0, off, :, jj * 2 * GRID_W:(jj + 1) * 2 * GRID_W] = pairs[off + 2 * jj]


def _na_bias(rpb):
    return pl.pallas_call(
        _na_bias_kernel,
        grid=(NA_HEADS,),
        in_specs=[pl.BlockSpec(memory_space=pltpu.SMEM)],
        out_specs=pl.BlockSpec((1, NA_BIAS_VARIANTS, GRID_W, NA_BAND_KEYS), lambda h: (h, 0, 0, 0)),
        out_shape=jax.ShapeDtypeStruct((NA_HEADS, NA_BIAS_VARIANTS, GRID_W, NA_BAND_KEYS), F32),
        compiler_params=_params("arbitrary"),
        name="na_bias",
    )(rpb.reshape(-1))


def _na_kernel(q_ref, k_ref, v_ref, bias_ref, o_ref, *, seq):
    rows = seq // GRID_W

    def group_body(gi, carry):
        qsls, ksls, scores = [], [], []
        for u in range(NA_ROW_GROUP):
            r = gi * NA_ROW_GROUP + u
            r0 = jnp.clip(r - NA_WIN_ROWS // 2, 0, rows - NA_WIN_ROWS)
            off = r0 - r + (NA_WIN_ROWS - 1)
            qsl = pl.ds(pl.multiple_of(r * GRID_W, GRID_W), GRID_W)
            ksl = pl.ds(pl.multiple_of(r0 * GRID_W, GRID_W), NA_BAND_KEYS)
            s = lax.dot_general(q_ref[qsl, :], k_ref[ksl, :], (((1,), (1,)), ((), ())),
                                preferred_element_type=F32)
            scores.append(s + bias_ref[0, off])
            qsls.append(qsl)
            ksls.append(ksl)
        probs, inv_ls = [], []
        for s in scores:
            p = jnp.exp(s - jnp.max(s, axis=-1, keepdims=True))
            inv_ls.append(1.0 / jnp.sum(p, axis=-1, keepdims=True))
            probs.append(p.astype(BF16))
        for qsl, ksl, p, inv_l in zip(qsls, ksls, probs, inv_ls):
            o = jnp.dot(p, v_ref[ksl, :], preferred_element_type=F32)
            o_ref[qsl, :] = (o * inv_l).astype(o_ref.dtype)
        return carry

    lax.fori_loop(0, rows // NA_ROW_GROUP, group_body, 0)


def _na_attention(qkv, bias, batch, seq):
    t = qkv.shape[0]
    return pl.pallas_call(
        functools.partial(_na_kernel, seq=seq),
        grid=(batch, NA_HEADS),
        in_specs=[pl.BlockSpec((seq, HEAD_DIM), lambda b, h: (b, h)),
                  pl.BlockSpec((seq, HEAD_DIM), lambda b, h: (b, NA_HEADS + h)),
                  pl.BlockSpec((seq, HEAD_DIM), lambda b, h: (b, 2 * NA_HEADS + h)),
                  pl.BlockSpec((1, NA_BIAS_VARIANTS, GRID_W, NA_BAND_KEYS), lambda b, h: (h, 0, 0, 0))],
        out_specs=pl.BlockSpec((seq, HEAD_DIM), lambda b, h: (b, h)),
        out_shape=jax.ShapeDtypeStruct((t, NA_WIDTH), BF16),
        compiler_params=_params("parallel", "parallel"),
        name="na_attention",
    )(qkv, qkv, qkv, bias)


DIFF_K_ROWS = MM_TILE
DIFF_Q_ROWS = 256
DIFF_Q_UNROLL = 4


def _rope_tables(seq):
    half = ROT_DIM // 2
    inv_freq = ROPE_THETA ** (-jnp.arange(half, dtype=F32) / half)
    ang = jnp.arange(seq, dtype=F32)[:, None] * inv_freq[None, :]
    cos, sin = jnp.cos(ang), jnp.sin(ang)
    rest = DIFF_QK_DIM - ROT_DIM
    c = jnp.concatenate([cos, cos, jnp.ones((seq, rest), F32)], axis=1)
    sa = jnp.concatenate([-sin, jnp.zeros((seq, DIFF_QK_DIM - half), F32)], axis=1)
    sb = jnp.concatenate([jnp.zeros((seq, half), F32), sin, jnp.zeros((seq, rest), F32)], axis=1)
    k_tabs = jnp.stack([c, sa, sb])
    return jnp.stack([k_tabs * (DIFF_QK_DIM ** -0.5 * LOG2_E), k_tabs])


def _diff_kernel(lam_ref, q_ref, k_ref, vt_ref, lq1_ref, lk1_ref, lq2_ref, lk2_ref, g_ref, o_ref, acc_ref, s_ref,
                 *, seq):
    bk = DIFF_K_ROWS
    bq = DIFF_Q_ROWS
    n_chunks = seq // bk
    n_q = seq // bq
    assert n_chunks % 2 == 0
    csl = [slice(c * DIFF_QK_DIM, (c + 1) * DIFF_QK_DIM) for c in range(2)]
    lam_init = lam_ref[0]
    lam = (jnp.exp(jnp.sum(lq1_ref[...] * lk1_ref[...], axis=-1, keepdims=True))
           - jnp.exp(jnp.sum(lq2_ref[...] * lk2_ref[...], axis=-1, keepdims=True)) + lam_init)

    def scores(qi, j, slot):
        qsl = pl.ds(pl.multiple_of(qi * bq, bq), bq)
        for c in range(2):
            s_ref[slot, c] = lax.dot_general(k_ref[j * bk:(j + 1) * bk, csl[c]], q_ref[qsl, csl[c]],
                                             (((1,), (1,)), ((), ())), preferred_element_type=F32)

    def consume(j, slot, carry):
        vt = vt_ref[0, j]
        new_ms, new_ls = [], []
        for c in range(2):
            st = s_ref[slot, c]
            m_blk = jnp.max(st, axis=0, keepdims=True)
            if j == 0:
                m_new = m_blk
                p = jnp.exp2(st - m_new)
                new_ls.append(jnp.sum(p, axis=0, keepdims=True))
                acc_ref[c] = jnp.dot(vt, p.astype(BF16), preferred_element_type=F32)
            else:
                ms, ls = carry
                m_new = jnp.maximum(ms[c], m_blk)
                alpha = jnp.exp2(ms[c] - m_new)
                p = jnp.exp2(st - m_new)
                new_ls.append(alpha * ls[c] + jnp.sum(p, axis=0, keepdims=True))
                acc_ref[c] = alpha * acc_ref[c] + jnp.dot(vt, p.astype(BF16), preferred_element_type=F32)
            new_ms.append(m_new)
        return tuple(new_ms), tuple(new_ls)

    def q_block(qi, _):
        carry = None
        for j in range(n_chunks):
            slot = j % 2
            if j + 1 < n_chunks:
                scores(qi, j + 1, 1 - slot)
            else:
                scores(jnp.minimum(qi + 1, n_q - 1), 0, 1 - slot)
            carry = consume(j, slot, carry)
        _, ls = carry
        o = acc_ref[0] * (1.0 / ls[0]) - lam * (acc_ref[1] * (1.0 / ls[1]))
        ms = jnp.mean(o * o, axis=0, keepdims=True)
        y = (o * lax.rsqrt(ms + NORM_EPS)).T * g_ref[...]
        o_ref[pl.ds(pl.multiple_of(qi * bq, bq), bq), :] = (y * lam_ref[1]).astype(o_ref.dtype)
        return 0

    scores(0, 0, 0)
    lax.fori_loop(0, n_q, q_block, 0, unroll=DIFF_Q_UNROLL)


def _diff_attention(qk, vt, lam_consts, lq1, lk1, lq2, lk2, subln_g, batch, seq):
    t = qk.shape[0]
    bq = DIFF_Q_ROWS
    nkb = seq // DIFF_K_ROWS
    vec = lambda n: pl.BlockSpec((1, n), lambda b, h: (0, 0))
    return pl.pallas_call(
        functools.partial(_diff_kernel, seq=seq),
        grid=(batch, DIFF_HEADS),
        in_specs=[pl.BlockSpec(memory_space=pltpu.SMEM),
                  pl.BlockSpec((seq, DIFF_V_DIM), lambda b, h: (b, h)),
                  pl.BlockSpec((seq, DIFF_V_DIM), lambda b, h: (b, DIFF_HEADS + h)),
                  pl.BlockSpec((1, nkb, DIFF_V_DIM, DIFF_K_ROWS), lambda b, h: (b, 0, h, 0)),
                  vec(DIFF_QK_DIM), vec(DIFF_QK_DIM), vec(DIFF_QK_DIM), vec(DIFF_QK_DIM), vec(DIFF_V_DIM)],
        out_specs=pl.BlockSpec((seq, DIFF_V_DIM), lambda b, h: (b, h)),
        out_shape=jax.ShapeDtypeStruct((t, DIFF_WIDTH), BF16),
        scratch_shapes=[pltpu.VMEM((2, DIFF_V_DIM, bq), F32),
                        pltpu.VMEM((2, 2, DIFF_K_ROWS, bq), F32)],
        compiler_params=_params("parallel", "parallel"),
        name="diff_attention",
    )(lam_consts, qk, qk, vt, lq1.reshape(1, -1), lk1.reshape(1, -1), lq2.reshape(1, -1), lk2.reshape(1, -1),
      subln_g.reshape(1, -1))


CONV_CHUNK = 512


def _conv_kernel(u_ref, b_ref, c_ref, w_ref, o_ref, *, seq):
    n_chunks = seq // CONV_CHUNK
    w0, w1, w2 = w_ref[0:1, :], w_ref[1:2, :], w_ref[2:3, :]
    row = lax.broadcasted_iota(jnp.int32, (CONV_CHUNK, LANES), 0)

    def body(ci, carry):
        t0 = pl.multiple_of(ci * CONV_CHUNK, CONV_CHUNK)
        sl = pl.ds(t0, CONV_CHUNK)
        z = c_ref[sl, :] * u_ref[sl, :]
        psl = pl.ds(pl.multiple_of(jnp.maximum(t0 - 8, 0), 8), 8)
        nsl = pl.ds(pl.multiple_of(jnp.minimum(t0 + CONV_CHUNK, seq - 8), 8), 8)
        zp = (c_ref[psl, :] * u_ref[psl, :])[7:8, :]
        zn = (c_ref[nsl, :] * u_ref[nsl, :])[0:1, :]
        zp = jnp.where(ci > 0, zp, 0.0)
        zn = jnp.where(ci < n_chunks - 1, zn, 0.0)
        z_prev = jnp.where(row == 0, zp, pltpu.roll(z, 1, 0))
        z_next = jnp.where(row == CONV_CHUNK - 1, zn, pltpu.roll(z, CONV_CHUNK - 1, 0))
        y = w0 * z_prev + w1 * z + w2 * z_next
        o_ref[sl, :] = (b_ref[sl, :] * y).astype(o_ref.dtype)
        return carry

    lax.fori_loop(0, n_chunks, body, 0)


def _short_conv(ubc, conv_w, batch, seq):
    t = ubc.shape[0]
    nb = CONV_WIDTH // LANES
    return pl.pallas_call(
        functools.partial(_conv_kernel, seq=seq),
        grid=(batch, nb),
        in_specs=[pl.BlockSpec((seq, LANES), lambda b, j: (b, j)),
                  pl.BlockSpec((seq, LANES), lambda b, j: (b, nb + j)),
                  pl.BlockSpec((seq, LANES), lambda b, j: (b, 2 * nb + j)),
                  pl.BlockSpec((3, LANES), lambda b, j: (0, j))],
        out_specs=pl.BlockSpec((seq, LANES), lambda b, j: (b, j)),
        out_shape=jax.ShapeDtypeStruct((t, CONV_WIDTH), BF16),
        compiler_params=_params("parallel", "parallel"),
        name="short_conv",
    )(ubc, ubc, ubc, conv_w)


def _na_col_scale():
    return jnp.concatenate([jnp.full((1, NA_WIDTH), HEAD_DIM ** -0.5, F32), jnp.ones((1, 2 * NA_WIDTH), F32)], axis=1)


WEIGHT_CAST_ROWS = 2048


def _cast_weight_kernel(src_ref, dst_ref):
    dst_ref[...] = src_ref[...].astype(dst_ref.dtype)


def _cast_weight(w, layer, col0=0, width=None):
    _, rows, cols = w.shape
    width = cols if width is None else width
    rb, cb = WEIGHT_CAST_ROWS, MM_TILE
    cb0 = col0 // cb
    return pl.pallas_call(
        _cast_weight_kernel,
        grid=(rows // rb, width // cb),
        in_specs=[pl.BlockSpec((None, rb, cb), lambda i, j: (layer, i, cb0 + j))],
        out_specs=pl.BlockSpec((rb, cb), lambda i, j: (i, j)),
        out_shape=jax.ShapeDtypeStruct((rows, width), BF16),
        compiler_params=_params("parallel", "parallel"),
        name="cast_weight",
    )(w)


IN_SECTIONS = ((NA_COL, 3 * NA_WIDTH), (DIFF_Q_COL, 2 * DIFF_WIDTH), (DIFF_V_COL, DIFF_WIDTH), (CONV_COL, 3 * CONV_WIDTH))


def _trunk_layer(x, xg, ssq, layer_idx, batch, seq, rope_tabs, next_norm1_g, weights, cast_srcs, na_bias, lq1, lk1,
                 lq2, lk2, subln_g, conv_w, norm2_g):
    w_in_sections, w_out, w_up, w_down = weights
    do_cast = bool(cast_srcs) and next_norm1_g is not None
    nsb = seq // MM_TILE
    tiles_per_qk = DIFF_WIDTH // MM_TILE
    modes = (("colscale", BF16, _na_col_scale(), pl.BlockSpec((1, MM_TILE), lambda i, j: (0, j)), "in_proj_na"),
             ("rope", BF16, rope_tabs, pl.BlockSpec((None, 3, MM_TILE, DIFF_QK_DIM),
                                                   lambda i, j: (j // tiles_per_qk, 0, i % nsb, 0)), "in_proj_qk"),
             ("transpose", BF16, None, None, "in_proj_vt"),
             ("f32", F32, None, None, "in_proj_conv"))
    projected, next_in_sections = [], []
    for w_sec, (col0, _), (mode, dtype, extra, extra_spec, name) in zip(w_in_sections, IN_SECTIONS, modes):
        cast = dict(cast_src=cast_srcs[0], cast_layer=layer_idx + 1, cast_col0=col0) if do_cast else {}
        out, w_next = _project(xg, ssq, w_sec, mode, dtype, extra=extra, extra_spec=extra_spec, batch=batch,
                               name=name, **cast)
        projected.append(out)
        next_in_sections.append(w_next)
    na_qkv, diff_qk, diff_vt, conv_ubc = projected

    na_out = _na_attention(na_qkv, na_bias, batch, seq)
    lam_init = 0.8 - 0.6 * math.exp(-0.3 * layer_idx)
    lam_consts = jnp.array([lam_init, 1.0 - lam_init], F32)
    diff_out = _diff_attention(diff_qk, diff_vt, lam_consts, lq1, lk1, lq2, lk2, subln_g, batch, seq)
    conv_out = _short_conv(conv_ubc, conv_w, batch, seq)

    x, xg2, ssq2 = _out_proj([na_out, diff_out, conv_out], w_out, x, norm2_g)
    if do_cast:
        act, next_rest = _mlp_up(xg2, ssq2, w_up, cast_layer=layer_idx + 1, cast_srcs=cast_srcs[1:])
        next_weights = (tuple(next_in_sections),) + next_rest
    else:
        act, _ = _mlp_up(xg2, ssq2, w_up)
        next_weights = None
    if next_norm1_g is None:
        (x,) = _mlp_down(act, w_down, x)
        return (x, None, None), next_weights
    return tuple(_mlp_down(act, w_down, x, g=next_norm1_g)), next_weights


def _run_trunk(x, norm1_g, final_norm_g, layer_params, layer_weights, cast_srcs=()):
    batch, seq, d = x.shape
    x = x.reshape(batch * seq, d)
    rope_tabs = _rope_tables(seq)
    xg, ssq = _norm_entry(x, norm1_g[0])
    layer_weights = list(layer_weights)
    for i, lp in enumerate(layer_params):
        next_g = norm1_g[i + 1] if i + 1 < DEPTH else None
        (x, xg, ssq), next_weights = _trunk_layer(x, xg, ssq, i, batch, seq, rope_tabs, next_g, layer_weights[i],
                                                  cast_srcs, *lp)
        if next_weights is not None:
            layer_weights.append(next_weights)
    return _rmsnorm(x, final_norm_g, F32).reshape(batch, seq, d), layer_weights


def kernel(x_prompt, x_sample, norm1_g, w_in, na_rpb, diff_lq1, diff_lk1, diff_lq2, diff_lk2, diff_subln_g,
           conv_w, w_out, norm2_g, w_up, w_down, final_norm_g):
    layer_params = [(_na_bias(na_rpb[i]), diff_lq1[i], diff_lk1[i], diff_lq2[i], diff_lk2[i], diff_subln_g[i],
                     conv_w[i], norm2_g[i]) for i in range(DEPTH)]
    first_weights = (tuple(_cast_weight(w_in, 0, c0, width) for c0, width in IN_SECTIONS),
                     _cast_weight(w_out, 0), _cast_weight(w_up, 0), _cast_weight(w_down, 0))
    y_prompt, layer_weights = _run_trunk(x_prompt, norm1_g, final_norm_g, layer_params, [first_weights],
                                         cast_srcs=(w_in, w_out, w_up, w_down))
    y_sample, _ = _run_trunk(x_sample, norm1_g, final_norm_g, layer_params, layer_weights)
    return (y_prompt, y_sample)
```

```python
import functools
import math

import jax
import jax.numpy as jnp
from jax import lax
from jax.experimental import pallas as pl
from jax.experimental.pallas import tpu as pltpu

F32 = jnp.float32
BF16 = jnp.bfloat16

D_MODEL = 4096
DEPTH = 4
GRID_W = 64
HEAD_DIM = 128
NA_HEADS = D_MODEL // 512
NA_WIDTH = NA_HEADS * HEAD_DIM
NA_WIN_ROWS = 8
NA_WIN_COLS = 16
DIFF_HEADS = D_MODEL // 512
DIFF_QK_DIM = HEAD_DIM
DIFF_V_DIM = 2 * DIFF_QK_DIM
DIFF_WIDTH = DIFF_HEADS * DIFF_V_DIM
CONV_WIDTH = D_MODEL - NA_WIDTH - DIFF_WIDTH
IN_WIDTH = 3 * NA_WIDTH + 3 * DIFF_WIDTH + 3 * CONV_WIDTH
D_FF = 4 * D_MODEL
ROPE_THETA = 500000.0
ROT_DIM = DIFF_QK_DIM // 4
NORM_EPS = 1e-5
NEG_INF = -1e30
LOG2_E = 1.4426950408889634

NA_COL = 0
DIFF_Q_COL = 3 * NA_WIDTH
DIFF_K_COL = DIFF_Q_COL + DIFF_WIDTH
DIFF_V_COL = DIFF_K_COL + DIFF_WIDTH
CONV_COL = DIFF_V_COL + DIFF_WIDTH

V7X_VMEM_BYTES = 64 * 1024 * 1024
V7X_VMEM_LIMIT_BYTES = V7X_VMEM_BYTES - 8 * 1024 * 1024
V7X_VMEM_LIMIT_WIDE_BYTES = V7X_VMEM_BYTES - 2 * 1024 * 1024
LANES = 128
MM_TILE = 1024


def _params(*sem, vmem_limit_bytes=V7X_VMEM_LIMIT_BYTES):
    return pltpu.CompilerParams(dimension_semantics=sem, vmem_limit_bytes=vmem_limit_bytes)


def _rmsnorm_kernel(x_ref, g_ref, o_ref):
    x = x_ref[...]
    ms = jnp.mean(x * x, axis=-1, keepdims=True)
    o_ref[...] = ((x * lax.rsqrt(ms + NORM_EPS)) * g_ref[...]).astype(o_ref.dtype)


def _rmsnorm(x, g, out_dtype, block_rows=256):
    t, d = x.shape
    return pl.pallas_call(
        _rmsnorm_kernel,
        grid=(t // block_rows,),
        in_specs=[pl.BlockSpec((block_rows, d), lambda i: (i, 0)),
                  pl.BlockSpec((1, d), lambda i: (0, 0))],
        out_specs=pl.BlockSpec((block_rows, d), lambda i: (i, 0)),
        out_shape=jax.ShapeDtypeStruct((t, d), out_dtype),
        compiler_params=_params("parallel"),
        name="rmsnorm",
    )(x, g.reshape(1, d))


def _norm_entry_kernel(x_ref, g_ref, xg_ref, ssq_ref):
    x = x_ref[...]
    xg_ref[...] = (x * g_ref[...]).astype(BF16)
    ssq_ref[...] = jnp.sum(x * x, axis=-1, keepdims=True)


def _norm_entry(x, g, block_rows=256):
    t, d = x.shape
    return pl.pallas_call(
        _norm_entry_kernel,
        grid=(t // block_rows,),
        in_specs=[pl.BlockSpec((block_rows, d), lambda i: (i, 0)),
                  pl.BlockSpec((1, d), lambda i: (0, 0))],
        out_specs=[pl.BlockSpec((block_rows, d), lambda i: (i, 0)),
                   pl.BlockSpec((block_rows, 1), lambda i: (i, 0))],
        out_shape=[jax.ShapeDtypeStruct((t, d), BF16), jax.ShapeDtypeStruct((t, 1), F32)],
        compiler_params=_params("parallel"),
        name="norm_entry",
    )(x, g.reshape(1, d))


def _row_rscale(ssq_ref):
    return lax.rsqrt(ssq_ref[...] * (1.0 / D_MODEL) + NORM_EPS)


def _rope_lanes(x, c, sa, sb):
    half = ROT_DIM // 2
    return x * c + pltpu.roll(x, DIFF_QK_DIM - half, 1) * sa + pltpu.roll(x, half, 1) * sb


def _proj_kernel(*refs, mode, has_extra, has_cast):
    xg_ref, w_ref, ssq_ref = refs[:3]
    extra_ref = refs[3] if has_extra else None
    o_ref = refs[3 + has_extra + has_cast]
    if has_cast:
        refs[-1][...] = refs[3 + has_extra][...].astype(refs[-1].dtype)
    acc = jnp.dot(xg_ref[...], w_ref[...], preferred_element_type=F32) * _row_rscale(ssq_ref)
    groups = [slice(g * LANES, (g + 1) * LANES) for g in range(acc.shape[1] // LANES)]
    if mode == "f32":
        o_ref[...] = acc
    elif mode == "colscale":
        o_ref[...] = (acc * extra_ref[...]).astype(o_ref.dtype)
    elif mode == "rope":
        c, sa, sb = extra_ref[0], extra_ref[1], extra_ref[2]
        for sl in groups:
            o_ref[:, sl] = _rope_lanes(acc[:, sl], c, sa, sb).astype(o_ref.dtype)
    elif mode == "transpose":
        for sl in groups:
            o_ref[0, 0, sl, :] = acc[:, sl].T.astype(o_ref.dtype)
    else:
        raise ValueError(mode)


def _project(xg, ssq, w, mode, out_dtype, extra=None, extra_spec=None, batch=None, cast_src=None, cast_layer=None,
             cast_col0=0, name="proj"):
    m, k = xg.shape
    n = w.shape[1]
    bm = bn = MM_TILE
    gm = m // bm
    in_specs = [pl.BlockSpec((bm, k), lambda i, j: (i, 0)),
                pl.BlockSpec((k, bn), lambda i, j: (0, j)),
                pl.BlockSpec((bm, 1), lambda i, j: (i, 0))]
    args = [xg, w, ssq]
    if extra is not None:
        in_specs.append(extra_spec)
        args.append(extra)
    if mode == "transpose":
        nsb = m // batch // bm
        out_specs = [pl.BlockSpec((1, 1, bn, bm), lambda i, j: (i // nsb, i % nsb, j, 0))]
        out_shape = [jax.ShapeDtypeStruct((batch, nsb, n, bm), out_dtype)]
    else:
        out_specs = [pl.BlockSpec((bm, bn), lambda i, j: (i, j))]
        out_shape = [jax.ShapeDtypeStruct((m, n), out_dtype)]
    if cast_src is not None:
        assert k % gm == 0 and cast_col0 % bn == 0
        rb = k // gm
        cb0 = cast_col0 // bn
        in_specs.append(pl.BlockSpec((None, rb, bn), lambda i, j: (cast_layer, i, cb0 + j)))
        args.append(cast_src)
        out_specs.append(pl.BlockSpec((rb, bn), lambda i, j: (i, j)))
        out_shape.append(jax.ShapeDtypeStruct((k, n), BF16))
    outs = pl.pallas_call(
        functools.partial(_proj_kernel, mode=mode, has_extra=extra is not None, has_cast=cast_src is not None),
        grid=(gm, n // bn),
        in_specs=in_specs,
        out_specs=out_specs,
        out_shape=out_shape,
        compiler_params=_params("parallel", "arbitrary"),
        name=name,
    )(*args)
    return outs[0], (outs[1] if cast_src is not None else None)


CAST_TILE_ROWS = 512


def _mlp_up_kernel(*refs, n_cast):
    xg_ref, w_ref, ssq_ref = refs[:3]
    src_refs = refs[3:3 + n_cast]
    act_ref = refs[3 + n_cast]
    dst_refs = refs[4 + n_cast:]
    for src_ref, dst_ref in zip(src_refs, dst_refs):
        dst_ref[...] = src_ref[...].astype(dst_ref.dtype)
    acc = jnp.dot(xg_ref[...], w_ref[...], preferred_element_type=F32) * _row_rscale(ssq_ref)
    r = jnp.maximum(acc, 0.0)
    act_ref[...] = (r * r).astype(act_ref.dtype)


def _mlp_up(xg, ssq, w, cast_layer=None, cast_srcs=()):
    m, k = xg.shape
    n = w.shape[1]
    bm = bn = MM_TILE
    gm, gn = m // bm, n // bn
    steps = gm * gn
    tile = pl.BlockSpec((bm, bn), lambda i, j: (i, j))
    in_specs = [pl.BlockSpec((bm, k), lambda i, j: (i, 0)),
                pl.BlockSpec((k, bn), lambda i, j: (0, j)),
                pl.BlockSpec((bm, 1), lambda i, j: (i, 0))]
    out_specs = [tile]
    out_shape = [jax.ShapeDtypeStruct((m, n), BF16)]
    for src in cast_srcs:
        _, rows, cols = src.shape
        rb = CAST_TILE_ROWS
        cb = rows * cols // (steps * rb)
        ct = cols // cb
        assert cb % LANES == 0 and cols % cb == 0 and (rows // rb) * ct == steps
        in_specs.append(pl.BlockSpec((None, rb, cb),
                                     lambda i, j, ct=ct: (cast_layer, (i * gn + j) // ct, (i * gn + j) % ct)))
        out_specs.append(pl.BlockSpec((rb, cb), lambda i, j, ct=ct: ((i * gn + j) // ct, (i * gn + j) % ct)))
        out_shape.append(jax.ShapeDtypeStruct((rows, cols), BF16))
    limit = V7X_VMEM_LIMIT_WIDE_BYTES if cast_srcs else V7X_VMEM_LIMIT_BYTES
    outs = pl.pallas_call(
        functools.partial(_mlp_up_kernel, n_cast=len(cast_srcs)),
        grid=(gm, gn),
        in_specs=in_specs,
        out_specs=out_specs,
        out_shape=out_shape,
        compiler_params=_params("parallel", "arbitrary", vmem_limit_bytes=limit),
        name="mlp_up",
    )(xg, w, ssq, *cast_srcs)
    return outs[0], tuple(outs[1:])


def _emit_norm_inputs(x, g_ref, xg_ref, ssq_ref, first):
    xg_ref[...] = (x * g_ref[...]).astype(BF16)
    part = jnp.sum(x * x, axis=-1, keepdims=True)

    @pl.when(first)
    def _():
        ssq_ref[...] = part

    @pl.when(jnp.logical_not(first))
    def _():
        ssq_ref[...] += part


def _out_proj_kernel(a1_ref, a2_ref, a3_ref, w_ref, res_ref, g_ref, x_ref, xg_ref, ssq_ref, *, k_sizes):
    acc = None
    k0 = 0
    for a_ref, ks in zip((a1_ref, a2_ref, a3_ref), k_sizes):
        d = jnp.dot(a_ref[...], w_ref[k0:k0 + ks, :], preferred_element_type=F32)
        acc = d if acc is None else acc + d
        k0 += ks
    x = acc + res_ref[...]
    x_ref[...] = x
    _emit_norm_inputs(x, g_ref, xg_ref, ssq_ref, pl.program_id(1) == 0)


def _out_proj(a_list, w, res, g):
    m = a_list[0].shape[0]
    k_sizes = tuple(a.shape[1] for a in a_list)
    k, n = w.shape
    assert sum(k_sizes) == k
    bm = bn = MM_TILE
    tile = pl.BlockSpec((bm, bn), lambda i, j: (i, j))
    return pl.pallas_call(
        functools.partial(_out_proj_kernel, k_sizes=k_sizes),
        grid=(m // bm, n // bn),
        in_specs=[pl.BlockSpec((bm, ks), lambda i, j: (i, 0)) for ks in k_sizes]
        + [pl.BlockSpec((k, bn), lambda i, j: (0, j)), tile, pl.BlockSpec((1, bn), lambda i, j: (0, j))],
        out_specs=[tile, tile, pl.BlockSpec((bm, 1), lambda i, j: (i, 0))],
        out_shape=[jax.ShapeDtypeStruct((m, n), F32), jax.ShapeDtypeStruct((m, n), BF16),
                   jax.ShapeDtypeStruct((m, 1), F32)],
        compiler_params=_params("parallel", "arbitrary", vmem_limit_bytes=V7X_VMEM_LIMIT_WIDE_BYTES),
        name="out_proj",
    )(*a_list, w, res, g.reshape(1, n))


def _mlp_down_kernel(*refs, emit_norm):
    if emit_norm:
        a_ref, w_ref, res_ref, g_ref, o_ref, xg_ref, ssq_ref = refs
    else:
        a_ref, w_ref, res_ref, o_ref = refs
    kk = pl.program_id(2)

    @pl.when(kk == 0)
    def _():
        o_ref[...] = res_ref[...]

    if not emit_norm:
        o_ref[...] += jnp.dot(a_ref[...], w_ref[...], preferred_element_type=F32)
        return
    last = kk == pl.num_programs(2) - 1

    @pl.when(jnp.logical_not(last))
    def _():
        o_ref[...] += jnp.dot(a_ref[...], w_ref[...], preferred_element_type=F32)

    @pl.when(last)
    def _():
        x = o_ref[...] + jnp.dot(a_ref[...], w_ref[...], preferred_element_type=F32)
        o_ref[...] = x
        _emit_norm_inputs(x, g_ref, xg_ref, ssq_ref, pl.program_id(1) == 0)


def _mlp_down(a, w, res, g=None, bm=MM_TILE, bn=MM_TILE, bk=4096):
    m, k = a.shape
    n = w.shape[1]
    emit_norm = g is not None
    tile = pl.BlockSpec((bm, bn), lambda i, j, kk: (i, j))
    in_specs = [pl.BlockSpec((bm, bk), lambda i, j, kk: (i, kk)),
                pl.BlockSpec((bk, bn), lambda i, j, kk: (kk, j)),
                tile]
    args = [a, w, res]
    out_specs = [tile]
    out_shape = [jax.ShapeDtypeStruct((m, n), F32)]
    if emit_norm:
        in_specs.append(pl.BlockSpec((1, bn), lambda i, j, kk: (0, j)))
        args.append(g.reshape(1, n))
        out_specs += [tile, pl.BlockSpec((bm, 1), lambda i, j, kk: (i, 0))]
        out_shape += [jax.ShapeDtypeStruct((m, n), BF16), jax.ShapeDtypeStruct((m, 1), F32)]
    return pl.pallas_call(
        functools.partial(_mlp_down_kernel, emit_norm=emit_norm),
        grid=(m // bm, n // bn, k // bk),
        in_specs=in_specs,
        out_specs=out_specs,
        out_shape=out_shape,
        compiler_params=_params("parallel", "arbitrary", "arbitrary", vmem_limit_bytes=V7X_VMEM_LIMIT_WIDE_BYTES),
        name="mlp_down",
    )(*args)


NA_BIAS_VARIANTS = NA_WIN_ROWS
NA_BAND_KEYS = NA_WIN_ROWS * GRID_W
RPB_ROWS = 2 * NA_WIN_ROWS - 1
RPB_COLS = 2 * NA_WIN_COLS - 1
NA_ROW_GROUP = 16


def _na_bias_kernel(rpb_ref, o_ref):
    h = pl.program_id(0)
    shape = (GRID_W, 2 * GRID_W)
    lane = lax.broadcasted_iota(jnp.int32, shape, 1)
    cq = lax.broadcasted_iota(jnp.int32, shape, 0)
    ck = lane & (GRID_W - 1)
    upper = lane >= GRID_W
    dcl = jnp.clip(ck - cq, -(NA_WIN_COLS - 1), NA_WIN_COLS - 1) + (NA_WIN_COLS - 1)
    c0 = jnp.clip(cq - NA_WIN_COLS // 2, 0, GRID_W - NA_WIN_COLS)
    col_in = jnp.logical_and(ck >= c0, ck < c0 + NA_WIN_COLS)
    pairs = []
    for d in range(RPB_ROWS - 1):
        acc = jnp.zeros(shape, F32)
        for dc in range(RPB_COLS):
            lo = rpb_ref[(h * RPB_ROWS + d) * RPB_COLS + dc]
            hi = rpb_ref[(h * RPB_ROWS + d + 1) * RPB_COLS + dc]
            acc = jnp.where(dcl == dc, jnp.where(upper, hi, lo), acc)
        pairs.append(jnp.where(col_in, acc, NEG_INF))
    for off in range(NA_BIAS_VARIANTS):
        for jj in range(NA_WIN_ROWS // 2):
            o_ref[0, off, :, jj * 2 * GRID_W:(jj + 1) * 2 * GRID_W] = pairs[off + 2 * jj]


def _na_bias(rpb):
    return pl.pallas_call(
        _na_bias_kernel,
        grid=(NA_HEADS,),
        in_specs=[pl.BlockSpec(memory_space=pltpu.SMEM)],
        out_specs=pl.BlockSpec((1, NA_BIAS_VARIANTS, GRID_W, NA_BAND_KEYS), lambda h: (h, 0, 0, 0)),
        out_shape=jax.ShapeDtypeStruct((NA_HEADS, NA_BIAS_VARIANTS, GRID_W, NA_BAND_KEYS), F32),
        compiler_params=_params("arbitrary"),
        name="na_bias",
    )(rpb.reshape(-1))


def _na_kernel(q_ref, k_ref, v_ref, bias_ref, o_ref, *, seq):
    rows = seq // GRID_W

    def group_body(gi, carry):
        qsls, ksls, scores = [], [], []
        for u in range(NA_ROW_GROUP):
            r = gi * NA_ROW_GROUP + u
            r0 = jnp.clip(r - NA_WIN_ROWS // 2, 0, rows - NA_WIN_ROWS)
            off = r0 - r + (NA_WIN_ROWS - 1)
            qsl = pl.ds(pl.multiple_of(r * GRID_W, GRID_W), GRID_W)
            ksl = pl.ds(pl.multiple_of(r0 * GRID_W, GRID_W), NA_BAND_KEYS)
            s = lax.dot_general(q_ref[qsl, :], k_ref[ksl, :], (((1,), (1,)), ((), ())),
                                preferred_element_type=F32)
            scores.append(s + bias_ref[0, off])
            qsls.append(qsl)
            ksls.append(ksl)
        probs, inv_ls = [], []
        for s in scores:
            p = jnp.exp(s - jnp.max(s, axis=-1, keepdims=True))
            inv_ls.append(1.0 / jnp.sum(p, axis=-1, keepdims=True))
            probs.append(p.astype(BF16))
        for qsl, ksl, p, inv_l in zip(qsls, ksls, probs, inv_ls):
            o = jnp.dot(p, v_ref[ksl, :], preferred_element_type=F32)
            o_ref[qsl, :] = (o * inv_l).astype(o_ref.dtype)
        return carry

    lax.fori_loop(0, rows // NA_ROW_GROUP, group_body, 0)


def _na_attention(qkv, bias, batch, seq):
    t = qkv.shape[0]
    return pl.pallas_call(
        functools.partial(_na_kernel, seq=seq),
        grid=(batch, NA_HEADS),
        in_specs=[pl.BlockSpec((seq, HEAD_DIM), lambda b, h: (b, h)),
                  pl.BlockSpec((seq, HEAD_DIM), lambda b, h: (b, NA_HEADS + h)),
                  pl.BlockSpec((seq, HEAD_DIM), lambda b, h: (b, 2 * NA_HEADS + h)),
                  pl.BlockSpec((1, NA_BIAS_VARIANTS, GRID_W, NA_BAND_KEYS), lambda b, h: (h, 0, 0, 0))],
        out_specs=pl.BlockSpec((seq, HEAD_DIM), lambda b, h: (b, h)),
        out_shape=jax.ShapeDtypeStruct((t, NA_WIDTH), BF16),
        compiler_params=_params("parallel", "parallel"),
        name="na_attention",
    )(qkv, qkv, qkv, bias)


DIFF_K_ROWS = MM_TILE
DIFF_Q_ROWS = 256
DIFF_Q_UNROLL = 4


def _rope_tables(seq):
    half = ROT_DIM // 2
    inv_freq = ROPE_THETA ** (-jnp.arange(half, dtype=F32) / half)
    ang = jnp.arange(seq, dtype=F32)[:, None] * inv_freq[None, :]
    cos, sin = jnp.cos(ang), jnp.sin(ang)
    rest = DIFF_QK_DIM - ROT_DIM
    c = jnp.concatenate([cos, cos, jnp.ones((seq, rest), F32)], axis=1)
    sa = jnp.concatenate([-sin, jnp.zeros((seq, DIFF_QK_DIM - half), F32)], axis=1)
    sb = jnp.concatenate([jnp.zeros((seq, half), F32), sin, jnp.zeros((seq, rest), F32)], axis=1)
    k_tabs = jnp.stack([c, sa, sb])
    return jnp.stack([k_tabs * (DIFF_QK_DIM ** -0.5 * LOG2_E), k_tabs])


def _diff_kernel(lam_ref, q_ref, k_ref, vt_ref, lq1_ref, lk1_ref, lq2_ref, lk2_ref, g_ref, o_ref, acc_ref, s_ref,
                 *, seq):
    bk = DIFF_K_ROWS
    bq = DIFF_Q_ROWS
    n_chunks = seq // bk
    n_q = seq // bq
    assert n_chunks % 2 == 0
    csl = [slice(c * DIFF_QK_DIM, (c + 1) * DIFF_QK_DIM) for c in range(2)]
    lam_init = lam_ref[0]
    lam = (jnp.exp(jnp.sum(lq1_ref[...] * lk1_ref[...], axis=-1, keepdims=True))
           - jnp.exp(jnp.sum(lq2_ref[...] * lk2_ref[...], axis=-1, keepdims=True)) + lam_init)

    def scores(qi, j, slot):
        qsl = pl.ds(pl.multiple_of(qi * bq, bq), bq)
        for c in range(2):
            s_ref[slot, c] = lax.dot_general(k_ref[j * bk:(j + 1) * bk, csl[c]], q_ref[qsl, csl[c]],
                                             (((1,), (1,)), ((), ())), preferred_element_type=F32)

    def consume(j, slot, carry):
        vt = vt_ref[0, j]
        new_ms, new_ls = [], []
        for c in range(2):
            st = s_ref[slot, c]
            m_blk = jnp.max(st, axis=0, keepdims=True)
            if j == 0:
                m_new = m_blk
                p = jnp.exp2(st - m_new)
                new_ls.append(jnp.sum(p, axis=0, keepdims=True))
                acc_ref[c] = jnp.dot(vt, p.astype(BF16), preferred_element_type=F32)
            else:
                ms, ls = carry
                m_new = jnp.maximum(ms[c], m_blk)
                alpha = jnp.exp2(ms[c] - m_new)
                p = jnp.exp2(st - m_new)
                new_ls.append(alpha * ls[c] + jnp.sum(p, axis=0, keepdims=True))
                acc_ref[c] = alpha * acc_ref[c] + jnp.dot(vt, p.astype(BF16), preferred_element_type=F32)
            new_ms.append(m_new)
        return tuple(new_ms), tuple(new_ls)

    def q_block(qi, _):
        carry = None
        for j in range(n_chunks):
            slot = j % 2
            if j + 1 < n_chunks:
                scores(qi, j + 1, 1 - slot)
            else:
                scores(jnp.minimum(qi + 1, n_q - 1), 0, 1 - slot)
            carry = consume(j, slot, carry)
        _, ls = carry
        o = acc_ref[0] * (1.0 / ls[0]) - lam * (acc_ref[1] * (1.0 / ls[1]))
        ms = jnp.mean(o * o, axis=0, keepdims=True)
        y = (o * lax.rsqrt(ms + NORM_EPS)).T * g_ref[...]
        o_ref[pl.ds(pl.multiple_of(qi * bq, bq), bq), :] = (y * lam_ref[1]).astype(o_ref.dtype)
        return 0

    scores(0, 0, 0)
    lax.fori_loop(0, n_q, q_block, 0, unroll=DIFF_Q_UNROLL)


def _diff_attention(qk, vt, lam_consts, lq1, lk1, lq2, lk2, subln_g, batch, seq):
    t = qk.shape[0]
    bq = DIFF_Q_ROWS
    nkb = seq // DIFF_K_ROWS
    vec = lambda n: pl.BlockSpec((1, n), lambda b, h: (0, 0))
    return pl.pallas_call(
        functools.partial(_diff_kernel, seq=seq),
        grid=(batch, DIFF_HEADS),
        in_specs=[pl.BlockSpec(memory_space=pltpu.SMEM),
                  pl.BlockSpec((seq, DIFF_V_DIM), lambda b, h: (b, h)),
                  pl.BlockSpec((seq, DIFF_V_DIM), lambda b, h: (b, DIFF_HEADS + h)),
                  pl.BlockSpec((1, nkb, DIFF_V_DIM, DIFF_K_ROWS), lambda b, h: (b, 0, h, 0)),
                  vec(DIFF_QK_DIM), vec(DIFF_QK_DIM), vec(DIFF_QK_DIM), vec(DIFF_QK_DIM), vec(DIFF_V_DIM)],
        out_specs=pl.BlockSpec((seq, DIFF_V_DIM), lambda b, h: (b, h)),
        out_shape=jax.ShapeDtypeStruct((t, DIFF_WIDTH), BF16),
        scratch_shapes=[pltpu.VMEM((2, DIFF_V_DIM, bq), F32),
                        pltpu.VMEM((2, 2, DIFF_K_ROWS, bq), F32)],
        compiler_params=_params("parallel", "parallel"),
        name="diff_attention",
    )(lam_consts, qk, qk, vt, lq1.reshape(1, -1), lk1.reshape(1, -1), lq2.reshape(1, -1), lk2.reshape(1, -1),
      subln_g.reshape(1, -1))


CONV_CHUNK = 512


def _conv_kernel(u_ref, b_ref, c_ref, w_ref, o_ref, *, seq):
    n_chunks = seq // CONV_CHUNK
    w0, w1, w2 = w_ref[0:1, :], w_ref[1:2, :], w_ref[2:3, :]
    row = lax.broadcasted_iota(jnp.int32, (CONV_CHUNK, LANES), 0)

    def body(ci, carry):
        t0 = pl.multiple_of(ci * CONV_CHUNK, CONV_CHUNK)
        sl = pl.ds(t0, CONV_CHUNK)
        z = c_ref[sl, :] * u_ref[sl, :]
        psl = pl.ds(pl.multiple_of(jnp.maximum(t0 - 8, 0), 8), 8)
        nsl = pl.ds(pl.multiple_of(jnp.minimum(t0 + CONV_CHUNK, seq - 8), 8), 8)
        zp = (c_ref[psl, :] * u_ref[psl, :])[7:8, :]
        zn = (c_ref[nsl, :] * u_ref[nsl, :])[0:1, :]
        zp = jnp.where(ci > 0, zp, 0.0)
        zn = jnp.where(ci < n_chunks - 1, zn, 0.0)
        z_prev = jnp.where(row == 0, zp, pltpu.roll(z, 1, 0))
        z_next = jnp.where(row == CONV_CHUNK - 1, zn, pltpu.roll(z, CONV_CHUNK - 1, 0))
        y = w0 * z_prev + w1 * z + w2 * z_next
        o_ref[sl, :] = (b_ref[sl, :] * y).astype(o_ref.dtype)
        return carry

    lax.fori_loop(0, n_chunks, body, 0)


def _short_conv(ubc, conv_w, batch, seq):
    t = ubc.shape[0]
    nb = CONV_WIDTH // LANES
    return pl.pallas_call(
        functools.partial(_conv_kernel, seq=seq),
        grid=(batch, nb),
        in_specs=[pl.BlockSpec((seq, LANES), lambda b, j: (b, j)),
                  pl.BlockSpec((seq, LANES), lambda b, j: (b, nb + j)),
                  pl.BlockSpec((seq, LANES), lambda b, j: (b, 2 * nb + j)),
                  pl.BlockSpec((3, LANES), lambda b, j: (0, j))],
        out_specs=pl.BlockSpec((seq, LANES), lambda b, j: (b, j)),
        out_shape=jax.ShapeDtypeStruct((t, CONV_WIDTH), BF16),
        compiler_params=_params("parallel", "parallel"),
        name="short_conv",
    )(ubc, ubc, ubc, conv_w)


def _na_col_scale():
    return jnp.concatenate([jnp.full((1, NA_WIDTH), HEAD_DIM ** -0.5, F32), jnp.ones((1, 2 * NA_WIDTH), F32)], axis=1)


WEIGHT_CAST_ROWS = 2048


def _cast_weight_kernel(src_ref, dst_ref):
    dst_ref[...] = src_ref[...].astype(dst_ref.dtype)


def _cast_weight(w, layer, col0=0, width=None):
    _, rows, cols = w.shape
    width = cols if width is None else width
    rb, cb = WEIGHT_CAST_ROWS, MM_TILE
    cb0 = col0 // cb
    return pl.pallas_call(
        _cast_weight_kernel,
        grid=(rows // rb, width // cb),
        in_specs=[pl.BlockSpec((None, rb, cb), lambda i, j: (layer, i, cb0 + j))],
        out_specs=pl.BlockSpec((rb, cb), lambda i, j: (i, j)),
        out_shape=jax.ShapeDtypeStruct((rows, width), BF16),
        compiler_params=_params("parallel", "parallel"),
        name="cast_weight",
    )(w)


IN_SECTIONS = ((NA_COL, 3 * NA_WIDTH), (DIFF_Q_COL, 2 * DIFF_WIDTH), (DIFF_V_COL, DIFF_WIDTH), (CONV_COL, 3 * CONV_WIDTH))


def _trunk_layer(x, xg, ssq, layer_idx, batch, seq, rope_tabs, next_norm1_g, weights, cast_srcs, na_bias, lq1, lk1,
                 lq2, lk2, subln_g, conv_w, norm2_g):
    w_in_sections, w_out, w_up, w_down = weights
    do_cast = bool(cast_srcs) and next_norm1_g is not None
    nsb = seq // MM_TILE
    tiles_per_qk = DIFF_WIDTH // MM_TILE
    modes = (("colscale", BF16, _na_col_scale(), pl.BlockSpec((1, MM_TILE), lambda i, j: (0, j)), "in_proj_na"),
             ("rope", BF16, rope_tabs, pl.BlockSpec((None, 3, MM_TILE, DIFF_QK_DIM),
                                                   lambda i, j: (j // tiles_per_qk, 0, i % nsb, 0)), "in_proj_qk"),
             ("transpose", BF16, None, None, "in_proj_vt"),
             ("f32", F32, None, None, "in_proj_conv"))
    projected, next_in_sections = [], []
    for w_sec, (col0, _), (mode, dtype, extra, extra_spec, name) in zip(w_in_sections, IN_SECTIONS, modes):
        cast = dict(cast_src=cast_srcs[0], cast_layer=layer_idx + 1, cast_col0=col0) if do_cast else {}
        out, w_next = _project(xg, ssq, w_sec, mode, dtype, extra=extra, extra_spec=extra_spec, batch=batch,
                               name=name, **cast)
        projected.append(out)
        next_in_sections.append(w_next)
    na_qkv, diff_qk, diff_vt, conv_ubc = projected

    na_out = _na_attention(na_qkv, na_bias, batch, seq)
    lam_init = 0.8 - 0.6 * math.exp(-0.3 * layer_idx)
    lam_consts = jnp.array([lam_init, 1.0 - lam_init], F32)
    diff_out = _diff_attention(diff_qk, diff_vt, lam_consts, lq1, lk1, lq2, lk2, subln_g, batch, seq)
    conv_out = _short_conv(conv_ubc, conv_w, batch, seq)

    x, xg2, ssq2 = _out_proj([na_out, diff_out, conv_out], w_out, x, norm2_g)
    if do_cast:
        act, next_rest = _mlp_up(xg2, ssq2, w_up, cast_layer=layer_idx + 1, cast_srcs=cast_srcs[1:])
        next_weights = (tuple(next_in_sections),) + next_rest
    else:
        act, _ = _mlp_up(xg2, ssq2, w_up)
        next_weights = None
    if next_norm1_g is None:
        (x,) = _mlp_down(act, w_down, x)
        return (x, None, None), next_weights
    return tuple(_mlp_down(act, w_down, x, g=next_norm1_g)), next_weights


def _run_trunk(x, norm1_g, final_norm_g, layer_params, layer_weights, cast_srcs=()):
    batch, seq, d = x.shape
    x = x.reshape(batch * seq, d)
    rope_tabs = _rope_tables(seq)
    xg, ssq = _norm_entry(x, norm1_g[0])
    layer_weights = list(layer_weights)
    for i, lp in enumerate(layer_params):
        next_g = norm1_g[i + 1] if i + 1 < DEPTH else None
        (x, xg, ssq), next_weights = _trunk_layer(x, xg, ssq, i, batch, seq, rope_tabs, next_g, layer_weights[i],
                                                  cast_srcs, *lp)
        if next_weights is not None:
            layer_weights.append(next_weights)
    return _rmsnorm(x, final_norm_g, F32).reshape(batch, seq, d), layer_weights


def kernel(x_prompt, x_sample, norm1_g, w_in, na_rpb, diff_lq1, diff_lk1, diff_lq2, diff_lk2, diff_subln_g,
           conv_w, w_out, norm2_g, w_up, w_down, final_norm_g):
    layer_params = [(_na_bias(na_rpb[i]), diff_lq1[i], diff_lk1[i], diff_lq2[i], diff_lk2[i], diff_subln_g[i],
                     conv_w[i], norm2_g[i]) for i in range(DEPTH)]
    first_weights = (tuple(_cast_weight(w_in, 0, c0, width) for c0, width in IN_SECTIONS),
                     _cast_weight(w_out, 0), _cast_weight(w_up, 0), _cast_weight(w_down, 0))
    y_prompt, layer_weights = _run_trunk(x_prompt, norm1_g, final_norm_g, layer_params, [first_weights],
                                         cast_srcs=(w_in, w_out, w_up, w_down))
    y_sample, _ = _run_trunk(x_sample, norm1_g, final_norm_g, layer_params, layer_weights)
    return (y_prompt, y_sample)
```

```python
import functools
import math

import jax
import jax.numpy as jnp
from jax import lax
from jax.experimental import pallas as pl
from jax.experimental.pallas import tpu as pltpu

F32 = jnp.float32
BF16 = jnp.bfloat16

D_MODEL = 4096
DEPTH = 4
GRID_W = 64
HEAD_DIM = 128
NA_HEADS = D_MODEL // 512
NA_WIDTH = NA_HEADS * HEAD_DIM
NA_WIN_ROWS = 8
NA_WIN_COLS = 16
DIFF_HEADS = D_MODEL // 512
DIFF_QK_DIM = HEAD_DIM
DIFF_V_DIM = 2 * DIFF_QK_DIM
DIFF_WIDTH = DIFF_HEADS * DIFF_V_DIM
CONV_WIDTH = D_MODEL - NA_WIDTH - DIFF_WIDTH
IN_WIDTH = 3 * NA_WIDTH + 3 * DIFF_WIDTH + 3 * CONV_WIDTH
D_FF = 4 * D_MODEL
ROPE_THETA = 500000.0
ROT_DIM = DIFF_QK_DIM // 4
NORM_EPS = 1e-5
NEG_INF = -1e30
LOG2_E = 1.4426950408889634

NA_COL = 0
DIFF_Q_COL = 3 * NA_WIDTH
DIFF_K_COL = DIFF_Q_COL + DIFF_WIDTH
DIFF_V_COL = DIFF_K_COL + DIFF_WIDTH
CONV_COL = DIFF_V_COL + DIFF_WIDTH

V7X_VMEM_BYTES = 64 * 1024 * 1024
V7X_VMEM_LIMIT_BYTES = V7X_VMEM_BYTES - 8 * 1024 * 1024
V7X_VMEM_LIMIT_WIDE_BYTES = V7X_VMEM_BYTES - 2 * 1024 * 1024
LANES = 128
MM_TILE = 1024


def _params(*sem, vmem_limit_bytes=V7X_VMEM_LIMIT_BYTES):
    return pltpu.CompilerParams(dimension_semantics=sem, vmem_limit_bytes=vmem_limit_bytes)


def _rmsnorm_kernel(x_ref, g_ref, o_ref):
    x = x_ref[...]
    ms = jnp.mean(x * x, axis=-1, keepdims=True)
    o_ref[...] = ((x * lax.rsqrt(ms + NORM_EPS)) * g_ref[...]).astype(o_ref.dtype)


def _rmsnorm(x, g, out_dtype, block_rows=256):
    t, d = x.shape
    return pl.pallas_call(
        _rmsnorm_kernel,
        grid=(t // block_rows,),
        in_specs=[pl.BlockSpec((block_rows, d), lambda i: (i, 0)),
                  pl.BlockSpec((1, d), lambda i: (0, 0))],
        out_specs=pl.BlockSpec((block_rows, d), lambda i: (i, 0)),
        out_shape=jax.ShapeDtypeStruct((t, d), out_dtype),
        compiler_params=_params("parallel"),
        name="rmsnorm",
    )(x, g.reshape(1, d))


def _norm_entry_kernel(x_ref, g_ref, xg_ref, ssq_ref):
    x = x_ref[...]
    xg_ref[...] = (x * g_ref[...]).astype(BF16)
    ssq_ref[...] = jnp.sum(x * x, axis=-1, keepdims=True)


def _norm_entry(x, g, block_rows=256):
    t, d = x.shape
    return pl.pallas_call(
        _norm_entry_kernel,
        grid=(t // block_rows,),
        in_specs=[pl.BlockSpec((block_rows, d), lambda i: (i, 0)),
                  pl.BlockSpec((1, d), lambda i: (0, 0))],
        out_specs=[pl.BlockSpec((block_rows, d), lambda i: (i, 0)),
                   pl.BlockSpec((block_rows, 1), lambda i: (i, 0))],
        out_shape=[jax.ShapeDtypeStruct((t, d), BF16), jax.ShapeDtypeStruct((t, 1), F32)],
        compiler_params=_params("parallel"),
        name="norm_entry",
    )(x, g.reshape(1, d))


def _row_rscale(ssq_ref):
    return lax.rsqrt(ssq_ref[...] * (1.0 / D_MODEL) + NORM_EPS)


def _rope_lanes(x, c, sa, sb):
    half = ROT_DIM // 2
    return x * c + pltpu.roll(x, DIFF_QK_DIM - half, 1) * sa + pltpu.roll(x, half, 1) * sb


def _proj_kernel(*refs, mode, has_extra, has_cast):
    xg_ref, w_ref, ssq_ref = refs[:3]
    extra_ref = refs[3] if has_extra else None
    o_ref = refs[3 + has_extra + has_cast]
    if has_cast:
        refs[-1][...] = refs[3 + has_extra][...].astype(refs[-1].dtype)
    acc = jnp.dot(xg_ref[...], w_ref[...], preferred_element_type=F32) * _row_rscale(ssq_ref)
    groups = [slice(g * LANES, (g + 1) * LANES) for g in range(acc.shape[1] // LANES)]
    if mode == "cast":
        o_ref[...] = acc.astype(o_ref.dtype)
    elif mode == "colscale":
        o_ref[...] = (acc * extra_ref[...]).astype(o_ref.dtype)
    elif mode == "rope":
        c, sa, sb = extra_ref[0], extra_ref[1], extra_ref[2]
        for sl in groups:
            o_ref[:, sl] = _rope_lanes(acc[:, sl], c, sa, sb).astype(o_ref.dtype)
    elif mode == "transpose":
        for sl in groups:
            o_ref[0, 0, sl, :] = acc[:, sl].T.astype(o_ref.dtype)
    else:
        raise ValueError(mode)


def _project(xg, ssq, w, mode, out_dtype, extra=None, extra_spec=None, batch=None, cast_src=None, cast_layer=None,
             cast_col0=0, name="proj"):
    m, k = xg.shape
    n = w.shape[1]
    bm = bn = MM_TILE
    gm = m // bm
    in_specs = [pl.BlockSpec((bm, k), lambda i, j: (i, 0)),
                pl.BlockSpec((k, bn), lambda i, j: (0, j)),
                pl.BlockSpec((bm, 1), lambda i, j: (i, 0))]
    args = [xg, w, ssq]
    if extra is not None:
        in_specs.append(extra_spec)
        args.append(extra)
    if mode == "transpose":
        nsb = m // batch // bm
        out_specs = [pl.BlockSpec((1, 1, bn, bm), lambda i, j: (i // nsb, i % nsb, j, 0))]
        out_shape = [jax.ShapeDtypeStruct((batch, nsb, n, bm), out_dtype)]
    else:
        out_specs = [pl.BlockSpec((bm, bn), lambda i, j: (i, j))]
        out_shape = [jax.ShapeDtypeStruct((m, n), out_dtype)]
    if cast_src is not None:
        assert k % gm == 0 and cast_col0 % bn == 0
        rb = k // gm
        cb0 = cast_col0 // bn
        in_specs.append(pl.BlockSpec((None, rb, bn), lambda i, j: (cast_layer, i, cb0 + j)))
        args.append(cast_src)
        out_specs.append(pl.BlockSpec((rb, bn), lambda i, j: (i, j)))
        out_shape.append(jax.ShapeDtypeStruct((k, n), BF16))
    outs = pl.pallas_call(
        functools.partial(_proj_kernel, mode=mode, has_extra=extra is not None, has_cast=cast_src is not None),
        grid=(gm, n // bn),
        in_specs=in_specs,
        out_specs=out_specs,
        out_shape=out_shape,
        compiler_params=_params("parallel", "arbitrary"),
        name=name,
    )(*args)
    return outs[0], (outs[1] if cast_src is not None else None)


CAST_TILE_ROWS = 512


def _mlp_up_kernel(*refs, n_cast):
    xg_ref, w_ref, ssq_ref = refs[:3]
    src_refs = refs[3:3 + n_cast]
    act_ref = refs[3 + n_cast]
    dst_refs = refs[4 + n_cast:]
    for src_ref, dst_ref in zip(src_refs, dst_refs):
        dst_ref[...] = src_ref[...].astype(dst_ref.dtype)
    acc = jnp.dot(xg_ref[...], w_ref[...], preferred_element_type=F32) * _row_rscale(ssq_ref)
    r = jnp.maximum(acc, 0.0)
    act_ref[...] = (r * r).astype(act_ref.dtype)


def _mlp_up(xg, ssq, w, cast_layer=None, cast_srcs=()):
    m, k = xg.shape
    n = w.shape[1]
    bm = bn = MM_TILE
    gm, gn = m // bm, n // bn
    steps = gm * gn
    tile = pl.BlockSpec((bm, bn), lambda i, j: (i, j))
    in_specs = [pl.BlockSpec((bm, k), lambda i, j: (i, 0)),
                pl.BlockSpec((k, bn), lambda i, j: (0, j)),
                pl.BlockSpec((bm, 1), lambda i, j: (i, 0))]
    out_specs = [tile]
    out_shape = [jax.ShapeDtypeStruct((m, n), BF16)]
    for src in cast_srcs:
        _, rows, cols = src.shape
        rb = CAST_TILE_ROWS
        cb = rows * cols // (steps * rb)
        ct = cols // cb
        assert cb % LANES == 0 and cols % cb == 0 and (rows // rb) * ct == steps
        in_specs.append(pl.BlockSpec((None, rb, cb),
                                     lambda i, j, ct=ct: (cast_layer, (i * gn + j) // ct, (i * gn + j) % ct)))
        out_specs.append(pl.BlockSpec((rb, cb), lambda i, j, ct=ct: ((i * gn + j) // ct, (i * gn + j) % ct)))
        out_shape.append(jax.ShapeDtypeStruct((rows, cols), BF16))
    limit = V7X_VMEM_LIMIT_WIDE_BYTES if cast_srcs else V7X_VMEM_LIMIT_BYTES
    outs = pl.pallas_call(
        functools.partial(_mlp_up_kernel, n_cast=len(cast_srcs)),
        grid=(gm, gn),
        in_specs=in_specs,
        out_specs=out_specs,
        out_shape=out_shape,
        compiler_params=_params("parallel", "arbitrary", vmem_limit_bytes=limit),
        name="mlp_up",
    )(xg, w, ssq, *cast_srcs)
    return outs[0], tuple(outs[1:])


def _emit_norm_inputs(x, g_ref, xg_ref, ssq_ref, first):
    xg_ref[...] = (x * g_ref[...]).astype(BF16)
    part = jnp.sum(x * x, axis=-1, keepdims=True)

    @pl.when(first)
    def _():
        ssq_ref[...] = part

    @pl.when(jnp.logical_not(first))
    def _():
        ssq_ref[...] += part


def _out_proj_kernel(a1_ref, a2_ref, a3_ref, w_ref, res_ref, g_ref, x_ref, xg_ref, ssq_ref, *, k_sizes):
    acc = None
    k0 = 0
    for a_ref, ks in zip((a1_ref, a2_ref, a3_ref), k_sizes):
        d = jnp.dot(a_ref[...], w_ref[k0:k0 + ks, :], preferred_element_type=F32)
        acc = d if acc is None else acc + d
        k0 += ks
    x = acc + res_ref[...]
    x_ref[...] = x
    _emit_norm_inputs(x, g_ref, xg_ref, ssq_ref, pl.program_id(1) == 0)


def _out_proj(a_list, w, res, g):
    m = a_list[0].shape[0]
    k_sizes = tuple(a.shape[1] for a in a_list)
    k, n = w.shape
    assert sum(k_sizes) == k
    bm = bn = MM_TILE
    tile = pl.BlockSpec((bm, bn), lambda i, j: (i, j))
    return pl.pallas_call(
        functools.partial(_out_proj_kernel, k_sizes=k_sizes),
        grid=(m // bm, n // bn),
        in_specs=[pl.BlockSpec((bm, ks), lambda i, j: (i, 0)) for ks in k_sizes]
        + [pl.BlockSpec((k, bn), lambda i, j: (0, j)), tile, pl.BlockSpec((1, bn), lambda i, j: (0, j))],
        out_specs=[tile, tile, pl.BlockSpec((bm, 1), lambda i, j: (i, 0))],
        out_shape=[jax.ShapeDtypeStruct((m, n), F32), jax.ShapeDtypeStruct((m, n), BF16),
                   jax.ShapeDtypeStruct((m, 1), F32)],
        compiler_params=_params("parallel", "arbitrary", vmem_limit_bytes=V7X_VMEM_LIMIT_WIDE_BYTES),
        name="out_proj",
    )(*a_list, w, res, g.reshape(1, n))


def _mlp_down_kernel(*refs, emit_norm):
    if emit_norm:
        a_ref, w_ref, res_ref, g_ref, o_ref, xg_ref, ssq_ref = refs
    else:
        a_ref, w_ref, res_ref, o_ref = refs
    kk = pl.program_id(2)

    @pl.when(kk == 0)
    def _():
        o_ref[...] = res_ref[...]

    if not emit_norm:
        o_ref[...] += jnp.dot(a_ref[...], w_ref[...], preferred_element_type=F32)
        return
    last = kk == pl.num_programs(2) - 1

    @pl.when(jnp.logical_not(last))
    def _():
        o_ref[...] += jnp.dot(a_ref[...], w_ref[...], preferred_element_type=F32)

    @pl.when(last)
    def _():
        x = o_ref[...] + jnp.dot(a_ref[...], w_ref[...], preferred_element_type=F32)
        o_ref[...] = x
        _emit_norm_inputs(x, g_ref, xg_ref, ssq_ref, pl.program_id(1) == 0)


def _mlp_down(a, w, res, g=None, bm=MM_TILE, bn=MM_TILE, bk=4096):
    m, k = a.shape
    n = w.shape[1]
    emit_norm = g is not None
    tile = pl.BlockSpec((bm, bn), lambda i, j, kk: (i, j))
    in_specs = [pl.BlockSpec((bm, bk), lambda i, j, kk: (i, kk)),
                pl.BlockSpec((bk, bn), lambda i, j, kk: (kk, j)),
                tile]
    args = [a, w, res]
    out_specs = [tile]
    out_shape = [jax.ShapeDtypeStruct((m, n), F32)]
    if emit_norm:
        in_specs.append(pl.BlockSpec((1, bn), lambda i, j, kk: (0, j)))
        args.append(g.reshape(1, n))
        out_specs += [tile, pl.BlockSpec((bm, 1), lambda i, j, kk: (i, 0))]
        out_shape += [jax.ShapeDtypeStruct((m, n), BF16), jax.ShapeDtypeStruct((m, 1), F32)]
    return pl.pallas_call(
        functools.partial(_mlp_down_kernel, emit_norm=emit_norm),
        grid=(m // bm, n // bn, k // bk),
        in_specs=in_specs,
        out_specs=out_specs,
        out_shape=out_shape,
        compiler_params=_params("parallel", "arbitrary", "arbitrary", vmem_limit_bytes=V7X_VMEM_LIMIT_WIDE_BYTES),
        name="mlp_down",
    )(*args)


NA_BIAS_VARIANTS = NA_WIN_ROWS
NA_BAND_KEYS = NA_WIN_ROWS * GRID_W
RPB_ROWS = 2 * NA_WIN_ROWS - 1
RPB_COLS = 2 * NA_WIN_COLS - 1
NA_ROW_GROUP = 16


def _na_bias_kernel(rpb_ref, o_ref):
    h = pl.program_id(0)
    shape = (GRID_W, 2 * GRID_W)
    lane = lax.broadcasted_iota(jnp.int32, shape, 1)
    cq = lax.broadcasted_iota(jnp.int32, shape, 0)
    ck = lane & (GRID_W - 1)
    upper = lane >= GRID_W
    dcl = jnp.clip(ck - cq, -(NA_WIN_COLS - 1), NA_WIN_COLS - 1) + (NA_WIN_COLS - 1)
    c0 = jnp.clip(cq - NA_WIN_COLS // 2, 0, GRID_W - NA_WIN_COLS)
    col_in = jnp.logical_and(ck >= c0, ck < c0 + NA_WIN_COLS)
    pairs = []
    for d in range(RPB_ROWS - 1):
        acc = jnp.zeros(shape, F32)
        for dc in range(RPB_COLS):
            lo = rpb_ref[(h * RPB_ROWS + d) * RPB_COLS + dc]
            hi = rpb_ref[(h * RPB_ROWS + d + 1) * RPB_COLS + dc]
            acc = jnp.where(dcl == dc, jnp.where(upper, hi, lo), acc)
        pairs.append(jnp.where(col_in, acc, NEG_INF))
    for off in range(NA_BIAS_VARIANTS):
        for jj in range(NA_WIN_ROWS // 2):
            o_ref[0, off, :, jj * 2 * GRID_W:(jj + 1) * 2 * GRID_W] = pairs[off + 2 * jj]


def _na_bias(rpb):
    return pl.pallas_call(
        _na_bias_kernel,
        grid=(NA_HEADS,),
        in_specs=[pl.BlockSpec(memory_space=pltpu.SMEM)],
        out_specs=pl.BlockSpec((1, NA_BIAS_VARIANTS, GRID_W, NA_BAND_KEYS), lambda h: (h, 0, 0, 0)),
        out_shape=jax.ShapeDtypeStruct((NA_HEADS, NA_BIAS_VARIANTS, GRID_W, NA_BAND_KEYS), F32),
        compiler_params=_params("arbitrary"),
        name="na_bias",
    )(rpb.reshape(-1))


def _na_kernel(q_ref, k_ref, v_ref, bias_ref, o_ref, *, seq):
    rows = seq // GRID_W

    def group_body(gi, carry):
        qsls, ksls, scores = [], [], []
        for u in range(NA_ROW_GROUP):
            r = gi * NA_ROW_GROUP + u
            r0 = jnp.clip(r - NA_WIN_ROWS // 2, 0, rows - NA_WIN_ROWS)
            off = r0 - r + (NA_WIN_ROWS - 1)
            qsl = pl.ds(pl.multiple_of(r * GRID_W, GRID_W), GRID_W)
            ksl = pl.ds(pl.multiple_of(r0 * GRID_W, GRID_W), NA_BAND_KEYS)
            s = lax.dot_general(q_ref[qsl, :], k_ref[ksl, :], (((1,), (1,)), ((), ())),
                                preferred_element_type=F32)
            scores.append(s + bias_ref[0, off])
            qsls.append(qsl)
            ksls.append(ksl)
        probs, inv_ls = [], []
        for s in scores:
            p = jnp.exp(s - jnp.max(s, axis=-1, keepdims=True))
            inv_ls.append(1.0 / jnp.sum(p, axis=-1, keepdims=True))
            probs.append(p.astype(BF16))
        for qsl, ksl, p, inv_l in zip(qsls, ksls, probs, inv_ls):
            o = jnp.dot(p, v_ref[ksl, :], preferred_element_type=F32)
            o_ref[qsl, :] = (o * inv_l).astype(o_ref.dtype)
        return carry

    lax.fori_loop(0, rows // NA_ROW_GROUP, group_body, 0)


def _na_attention(qkv, bias, batch, seq):
    t = qkv.shape[0]
    return pl.pallas_call(
        functools.partial(_na_kernel, seq=seq),
        grid=(batch, NA_HEADS),
        in_specs=[pl.BlockSpec((seq, HEAD_DIM), lambda b, h: (b, h)),
                  pl.BlockSpec((seq, HEAD_DIM), lambda b, h: (b, NA_HEADS + h)),
                  pl.BlockSpec((seq, HEAD_DIM), lambda b, h: (b, 2 * NA_HEADS + h)),
                  pl.BlockSpec((1, NA_BIAS_VARIANTS, GRID_W, NA_BAND_KEYS), lambda b, h: (h, 0, 0, 0))],
        out_specs=pl.BlockSpec((seq, HEAD_DIM), lambda b, h: (b, h)),
        out_shape=jax.ShapeDtypeStruct((t, NA_WIDTH), BF16),
        compiler_params=_params("parallel", "parallel"),
        name="na_attention",
    )(qkv, qkv, qkv, bias)


DIFF_K_ROWS = MM_TILE
DIFF_Q_ROWS = 256
DIFF_Q_UNROLL = 4


def _rope_tables(seq):
    half = ROT_DIM // 2
    inv_freq = ROPE_THETA ** (-jnp.arange(half, dtype=F32) / half)
    ang = jnp.arange(seq, dtype=F32)[:, None] * inv_freq[None, :]
    cos, sin = jnp.cos(ang), jnp.sin(ang)
    rest = DIFF_QK_DIM - ROT_DIM
    c = jnp.concatenate([cos, cos, jnp.ones((seq, rest), F32)], axis=1)
    sa = jnp.concatenate([-sin, jnp.zeros((seq, DIFF_QK_DIM - half), F32)], axis=1)
    sb = jnp.concatenate([jnp.zeros((seq, half), F32), sin, jnp.zeros((seq, rest), F32)], axis=1)
    k_tabs = jnp.stack([c, sa, sb])
    return jnp.stack([k_tabs * (DIFF_QK_DIM ** -0.5 * LOG2_E), k_tabs])


def _diff_kernel(lam_ref, q_ref, k_ref, vt_ref, lq1_ref, lk1_ref, lq2_ref, lk2_ref, g_ref, o_ref, acc_ref, s_ref,
                 *, seq):
    bk = DIFF_K_ROWS
    bq = DIFF_Q_ROWS
    n_chunks = seq // bk
    n_q = seq // bq
    assert n_chunks % 2 == 0
    csl = [slice(c * DIFF_QK_DIM, (c + 1) * DIFF_QK_DIM) for c in range(2)]
    lam_init = lam_ref[0]
    lam = (jnp.exp(jnp.sum(lq1_ref[...] * lk1_ref[...], axis=-1, keepdims=True))
           - jnp.exp(jnp.sum(lq2_ref[...] * lk2_ref[...], axis=-1, keepdims=True)) + lam_init)

    def scores(qi, j, slot):
        qsl = pl.ds(pl.multiple_of(qi * bq, bq), bq)
        for c in range(2):
            s_ref[slot, c] = lax.dot_general(k_ref[j * bk:(j + 1) * bk, csl[c]], q_ref[qsl, csl[c]],
                                             (((1,), (1,)), ((), ())), preferred_element_type=F32)

    def consume(j, slot, carry):
        vt = vt_ref[0, j]
        new_ms, new_ls = [], []
        for c in range(2):
            st = s_ref[slot, c]
            m_blk = jnp.max(st, axis=0, keepdims=True)
            if j == 0:
                m_new = m_blk
                p = jnp.exp2(st - m_new)
                new_ls.append(jnp.sum(p, axis=0, keepdims=True))
                acc_ref[c] = jnp.dot(vt, p.astype(BF16), preferred_element_type=F32)
            else:
                ms, ls = carry
                m_new = jnp.maximum(ms[c], m_blk)
                alpha = jnp.exp2(ms[c] - m_new)
                p = jnp.exp2(st - m_new)
                new_ls.append(alpha * ls[c] + jnp.sum(p, axis=0, keepdims=True))
                acc_ref[c] = alpha * acc_ref[c] + jnp.dot(vt, p.astype(BF16), preferred_element_type=F32)
            new_ms.append(m_new)
        return tuple(new_ms), tuple(new_ls)

    def q_block(qi, _):
        carry = None
        for j in range(n_chunks):
            slot = j % 2
            if j + 1 < n_chunks:
                scores(qi, j + 1, 1 - slot)
            else:
                scores(jnp.minimum(qi + 1, n_q - 1), 0, 1 - slot)
            carry = consume(j, slot, carry)
        _, ls = carry
        o = acc_ref[0] * (1.0 / ls[0]) - lam * (acc_ref[1] * (1.0 / ls[1]))
        ms = jnp.mean(o * o, axis=0, keepdims=True)
        y = (o * lax.rsqrt(ms + NORM_EPS)).T * g_ref[...]
        o_ref[pl.ds(pl.multiple_of(qi * bq, bq), bq), :] = (y * lam_ref[1]).astype(o_ref.dtype)
        return 0

    scores(0, 0, 0)
    lax.fori_loop(0, n_q, q_block, 0, unroll=DIFF_Q_UNROLL)


def _diff_attention(qk, vt, lam_consts, lq1, lk1, lq2, lk2, subln_g, batch, seq):
    t = qk.shape[0]
    bq = DIFF_Q_ROWS
    nkb = seq // DIFF_K_ROWS
    vec = lambda n: pl.BlockSpec((1, n), lambda b, h: (0, 0))
    return pl.pallas_call(
        functools.partial(_diff_kernel, seq=seq),
        grid=(batch, DIFF_HEADS),
        in_specs=[pl.BlockSpec(memory_space=pltpu.SMEM),
                  pl.BlockSpec((seq, DIFF_V_DIM), lambda b, h: (b, h)),
                  pl.BlockSpec((seq, DIFF_V_DIM), lambda b, h: (b, DIFF_HEADS + h)),
                  pl.BlockSpec((1, nkb, DIFF_V_DIM, DIFF_K_ROWS), lambda b, h: (b, 0, h, 0)),
                  vec(DIFF_QK_DIM), vec(DIFF_QK_DIM), vec(DIFF_QK_DIM), vec(DIFF_QK_DIM), vec(DIFF_V_DIM)],
        out_specs=pl.BlockSpec((seq, DIFF_V_DIM), lambda b, h: (b, h)),
        out_shape=jax.ShapeDtypeStruct((t, DIFF_WIDTH), BF16),
        scratch_shapes=[pltpu.VMEM((2, DIFF_V_DIM, bq), F32),
                        pltpu.VMEM((2, 2, DIFF_K_ROWS, bq), F32)],
        compiler_params=_params("parallel", "parallel"),
        name="diff_attention",
    )(lam_consts, qk, qk, vt, lq1.reshape(1, -1), lk1.reshape(1, -1), lq2.reshape(1, -1), lk2.reshape(1, -1),
      subln_g.reshape(1, -1))


CONV_CHUNK = 512
CONV_HALO = 16


def _conv_kernel(u_ref, b_ref, c_ref, w_ref, o_ref, *, seq):
    n_chunks = seq // CONV_CHUNK
    w0, w1, w2 = w_ref[0:1, :], w_ref[1:2, :], w_ref[2:3, :]
    row = lax.broadcasted_iota(jnp.int32, (CONV_CHUNK, LANES), 0)

    def body(ci, carry):
        t0 = pl.multiple_of(ci * CONV_CHUNK, CONV_CHUNK)
        sl = pl.ds(t0, CONV_CHUNK)
        z = c_ref[sl, :].astype(F32) * u_ref[sl, :].astype(F32)
        psl = pl.ds(pl.multiple_of(jnp.maximum(t0 - CONV_HALO, 0), CONV_HALO), CONV_HALO)
        nsl = pl.ds(pl.multiple_of(jnp.minimum(t0 + CONV_CHUNK, seq - CONV_HALO), CONV_HALO), CONV_HALO)
        zp = (c_ref[psl, :].astype(F32) * u_ref[psl, :].astype(F32))[CONV_HALO - 1:CONV_HALO, :]
        zn = (c_ref[nsl, :].astype(F32) * u_ref[nsl, :].astype(F32))[0:1, :]
        zp = jnp.where(ci > 0, zp, 0.0)
        zn = jnp.where(ci < n_chunks - 1, zn, 0.0)
        z_prev = jnp.where(row == 0, zp, pltpu.roll(z, 1, 0))
        z_next = jnp.where(row == CONV_CHUNK - 1, zn, pltpu.roll(z, CONV_CHUNK - 1, 0))
        y = w0 * z_prev + w1 * z + w2 * z_next
        o_ref[sl, :] = (b_ref[sl, :].astype(F32) * y).astype(o_ref.dtype)
        return carry

    lax.fori_loop(0, n_chunks, body, 0)


def _short_conv(ubc, conv_w, batch, seq):
    t = ubc.shape[0]
    nb = CONV_WIDTH // LANES
    return pl.pallas_call(
        functools.partial(_conv_kernel, seq=seq),
        grid=(batch, nb),
        in_specs=[pl.BlockSpec((seq, LANES), lambda b, j: (b, j)),
                  pl.BlockSpec((seq, LANES), lambda b, j: (b, nb + j)),
                  pl.BlockSpec((seq, LANES), lambda b, j: (b, 2 * nb + j)),
                  pl.BlockSpec((3, LANES), lambda b, j: (0, j))],
        out_specs=pl.BlockSpec((seq, LANES), lambda b, j: (b, j)),
        out_shape=jax.ShapeDtypeStruct((t, CONV_WIDTH), BF16),
        compiler_params=_params("parallel", "parallel"),
        name="short_conv",
    )(ubc, ubc, ubc, conv_w)


def _na_col_scale():
    return jnp.concatenate([jnp.full((1, NA_WIDTH), HEAD_DIM ** -0.5, F32), jnp.ones((1, 2 * NA_WIDTH), F32)], axis=1)


WEIGHT_CAST_ROWS = 2048


def _cast_weight_kernel(src_ref, dst_ref):
    dst_ref[...] = src_ref[...].astype(dst_ref.dtype)


def _cast_weight(w, layer, col0=0, width=None):
    _, rows, cols = w.shape
    width = cols if width is None else width
    rb, cb = WEIGHT_CAST_ROWS, MM_TILE
    cb0 = col0 // cb
    return pl.pallas_call(
        _cast_weight_kernel,
        grid=(rows // rb, width // cb),
        in_specs=[pl.BlockSpec((None, rb, cb), lambda i, j: (layer, i, cb0 + j))],
        out_specs=pl.BlockSpec((rb, cb), lambda i, j: (i, j)),
        out_shape=jax.ShapeDtypeStruct((rows, width), BF16),
        compiler_params=_params("parallel", "parallel"),
        name="cast_weight",
    )(w)


IN_SECTIONS = ((NA_COL, 3 * NA_WIDTH), (DIFF_Q_COL, 2 * DIFF_WIDTH), (DIFF_V_COL, DIFF_WIDTH), (CONV_COL, 3 * CONV_WIDTH))


def _trunk_layer(x, xg, ssq, layer_idx, batch, seq, rope_tabs, next_norm1_g, weights, cast_srcs, na_bias, lq1, lk1,
                 lq2, lk2, subln_g, conv_w, norm2_g):
    w_in_sections, w_out, w_up, w_down = weights
    do_cast = bool(cast_srcs) and next_norm1_g is not None
    nsb = seq // MM_TILE
    tiles_per_qk = DIFF_WIDTH // MM_TILE
    modes = (("colscale", BF16, _na_col_scale(), pl.BlockSpec((1, MM_TILE), lambda i, j: (0, j)), "in_proj_na"),
             ("rope", BF16, rope_tabs, pl.BlockSpec((None, 3, MM_TILE, DIFF_QK_DIM),
                                                   lambda i, j: (j // tiles_per_qk, 0, i % nsb, 0)), "in_proj_qk"),
             ("transpose", BF16, None, None, "in_proj_vt"),
             ("cast", BF16, None, None, "in_proj_conv"))
    projected, next_in_sections = [], []
    for w_sec, (col0, _), (mode, dtype, extra, extra_spec, name) in zip(w_in_sections, IN_SECTIONS, modes):
        cast = dict(cast_src=cast_srcs[0], cast_layer=layer_idx + 1, cast_col0=col0) if do_cast else {}
        out, w_next = _project(xg, ssq, w_sec, mode, dtype, extra=extra, extra_spec=extra_spec, batch=batch,
                               name=name, **cast)
        projected.append(out)
        next_in_sections.append(w_next)
    na_qkv, diff_qk, diff_vt, conv_ubc = projected

    na_out = _na_attention(na_qkv, na_bias, batch, seq)
    lam_init = 0.8 - 0.6 * math.exp(-0.3 * layer_idx)
    lam_consts = jnp.array([lam_init, 1.0 - lam_init], F32)
    diff_out = _diff_attention(diff_qk, diff_vt, lam_consts, lq1, lk1, lq2, lk2, subln_g, batch, seq)
    conv_out = _short_conv(conv_ubc, conv_w, batch, seq)

    x, xg2, ssq2 = _out_proj([na_out, diff_out, conv_out], w_out, x, norm2_g)
    if do_cast:
        act, next_rest = _mlp_up(xg2, ssq2, w_up, cast_layer=layer_idx + 1, cast_srcs=cast_srcs[1:])
        next_weights = (tuple(next_in_sections),) + next_rest
    else:
        act, _ = _mlp_up(xg2, ssq2, w_up)
        next_weights = None
    if next_norm1_g is None:
        (x,) = _mlp_down(act, w_down, x)
        return (x, None, None), next_weights
    return tuple(_mlp_down(act, w_down, x, g=next_norm1_g)), next_weights


def _run_trunk(x, norm1_g, final_norm_g, layer_params, layer_weights, cast_srcs=()):
    batch, seq, d = x.shape
    x = x.reshape(batch * seq, d)
    rope_tabs = _rope_tables(seq)
    xg, ssq = _norm_entry(x, norm1_g[0])
    layer_weights = list(layer_weights)
    for i, lp in enumerate(layer_params):
        next_g = norm1_g[i + 1] if i + 1 < DEPTH else None
        (x, xg, ssq), next_weights = _trunk_layer(x, xg, ssq, i, batch, seq, rope_tabs, next_g, layer_weights[i],
                                                  cast_srcs, *lp)
        if next_weights is not None:
            layer_weights.append(next_weights)
    return _rmsnorm(x, final_norm_g, F32).reshape(batch, seq, d), layer_weights


def kernel(x_prompt, x_sample, norm1_g, w_in, na_rpb, diff_lq1, diff_lk1, diff_lq2, diff_lk2, diff_subln_g,
           conv_w, w_out, norm2_g, w_up, w_down, final_norm_g):
    layer_params = [(_na_bias(na_rpb[i]), diff_lq1[i], diff_lk1[i], diff_lq2[i], diff_lk2[i], diff_subln_g[i],
                     conv_w[i], norm2_g[i]) for i in range(DEPTH)]
    first_weights = (tuple(_cast_weight(w_in, 0, c0, width) for c0, width in IN_SECTIONS),
                     _cast_weight(w_out, 0), _cast_weight(w_up, 0), _cast_weight(w_down, 0))
    y_prompt, layer_weights = _run_trunk(x_prompt, norm1_g, final_norm_g, layer_params, [first_weights],
                                         cast_srcs=(w_in, w_out, w_up, w_down))
    y_sample, _ = _run_trunk(x_sample, norm1_g, final_norm_g, layer_params, layer_weights)
    return (y_prompt, y_sample)
```
